```python
import jax, jax.numpy as jnp
from jax import lax
import numpy as np

D_MODEL = 1024
BATCH = 8
SEQ = 4096
DEPTH = 4
DEC_BATCH = 16
DEC_SEQ = 32
PAST_LEN = 2048

CHUNK = 64
EPS = 1e-6
NEG_INF = -1e30
MIXER_PATTERN = ('gla', 'band_attn', 'rglru')

GLA_HEADS = 4
GLA_DK = D_MODEL // 2 // GLA_HEADS
GLA_DV = D_MODEL // GLA_HEADS
GLA_GATE_RANK = 16
GLA_GATE_TAU = 16.0
GLA_IN = 2 * GLA_HEADS * GLA_DK + 2 * GLA_HEADS * GLA_DV

ATT_HEADS = 16
ATT_DH = D_MODEL // ATT_HEADS
ATT_PREV_CHUNKS = 8
ATT_BAND = ATT_PREV_CHUNKS * CHUNK
ATT_MAX_REL = 128

LRU_WIDTH = ((4 * D_MODEL // 3 + 255) // 256) * 256
LRU_BLOCKS = 16
LRU_BLOCK = LRU_WIDTH // LRU_BLOCKS
CONV_WIDTH = 4
LRU_C = 8.0

D_FF = ((8 * D_MODEL // 3 + 255) // 256) * 256

kernel_name = 'hybrid_streaming_encoder_step'


def rmsnorm(x, g):
    xf = x.astype(jnp.float32)
    y = xf * lax.rsqrt(jnp.mean(xf * xf, axis=-1, keepdims=True) + EPS)
    return (y * g.astype(jnp.float32)).astype(x.dtype)


def swiglu_ffn(x, w_in, w_out):
    g, u = jnp.split(x @ w_in, 2, axis=-1)
    return (jax.nn.silu(g) * u) @ w_out


def gla_chunk(S, xs):
    qc, kc, vc, gc = xs
    C = qc.shape[2]
    b = jnp.cumsum(gc, axis=2)
    q_e = qc * jnp.exp(b)
    k_e = kc * jnp.exp(-b)
    causal = jnp.tril(jnp.ones((C, C), dtype=bool))
    a = jnp.where(causal, jnp.einsum('bhtd,bhsd->bhts', q_e, k_e), 0.0)
    o = jnp.einsum('bhts,bhsv->bhtv', a, vc) + jnp.einsum('bhtd,bhdv->bhtv', q_e, S)
    b_last = b[:, :, -1:, :]
    S_new = jnp.exp(b_last[:, :, 0, :])[..., None] * S + jnp.einsum(
        'bhsd,bhsv->bhdv', kc * jnp.exp(b_last - b), vc)
    return S_new, o


def gla_mixer(h, S0, w_in, w_gate_a, w_gate_b, b_gate, head_norm, w_out):
    B, T, _ = h.shape
    hk, hv = GLA_HEADS * GLA_DK, GLA_HEADS * GLA_DV
    q, k, v, r = jnp.split(h @ w_in, [hk, 2 * hk, 2 * hk + hv], axis=-1)
    g = jax.nn.log_sigmoid(((h @ w_gate_a) @ w_gate_b + b_gate).astype(jnp.float32)) / GLA_GATE_TAU

    def heads(t, d):
        return t.reshape(B, T, GLA_HEADS, d).transpose(0, 2, 1, 3).astype(jnp.float32)

    q = heads(q, GLA_DK) * GLA_DK ** -0.5
    k = heads(k, GLA_DK)
    v = heads(v, GLA_DV)
    g = heads(g, GLA_DK)
    C = min(T, CHUNK)
    nc = T // C

    def to_chunks(t):
        return jnp.moveaxis(t.reshape(B, GLA_HEADS, nc, C, t.shape[-1]), 2, 0)

    S, o = lax.scan(gla_chunk, S0.astype(jnp.float32),
                    (to_chunks(q), to_chunks(k), to_chunks(v), to_chunks(g)))
    o = jnp.moveaxis(o, 0, 2).reshape(B, GLA_HEADS, T, GLA_DV)
    o = o * lax.rsqrt(jnp.mean(o * o, axis=-1, keepdims=True) + EPS) * head_norm.astype(jnp.float32)
    o = o.transpose(0, 2, 1, 3).reshape(B, T, hv).astype(h.dtype)
    y = (o * jax.nn.silu(r)) @ w_out
    return y, S.astype(h.dtype)


def rel_bias(table, dist):
    idx = jnp.clip(dist, -ATT_MAX_REL, ATT_MAX_REL) + ATT_MAX_REL
    return table[:, idx].astype(jnp.float32)


def attend(q, k, v, bias, valid):
    s = jnp.einsum('bqhd,bkhd->bhqk', q, k).astype(jnp.float32) * ATT_DH ** -0.5 + bias
    s = jnp.where(valid, s, NEG_INF)
    p = jax.nn.softmax(s, axis=-1).astype(v.dtype)
    return jnp.einsum('bhqk,bkhd->bqhd', p, v)


def qkv_heads(h, w_qkv):
    B, T, _ = h.shape
    q, k, v = jnp.split(h @ w_qkv, 3, axis=-1)
    return (q.reshape(B, T, ATT_HEADS, ATT_DH), k.reshape(B, T, ATT_HEADS, ATT_DH),
            v.reshape(B, T, ATT_HEADS, ATT_DH))


def band_attn_prompt(h, w_qkv, rel_table, w_out):
    B, T, _ = h.shape
    q, k, v = qkv_heads(h, w_qkv)
    nc = T // CHUNK
    band = ATT_BAND + CHUNK
    pad = ((0, 0), (ATT_BAND, 0), (0, 0), (0, 0))
    kp = jnp.pad(k, pad)
    vp = jnp.pad(v, pad)
    t_idx = jnp.arange(CHUNK)[:, None]
    r_idx = jnp.arange(band)[None, :]
    bias = rel_bias(rel_table, ATT_BAND + t_idx - r_idx)
    q_chunks = jnp.moveaxis(q.reshape(B, nc, CHUNK, ATT_HEADS, ATT_DH), 1, 0)

    def one_chunk(args):
        c, qc = args
        kb = lax.dynamic_slice_in_dim(kp, c * CHUNK, band, axis=1)
        vb = lax.dynamic_slice_in_dim(vp, c * CHUNK, band, axis=1)
        valid = r_idx[0] >= ATT_BAND - c * CHUNK
        return attend(qc, kb, vb, bias, valid)

    o = lax.map(one_chunk, (jnp.arange(nc), q_chunks))
    o = jnp.moveaxis(o, 0, 1).reshape(B, T, D_MODEL)
    keep = min(ATT_BAND, T)
    return o @ w_out, k[:, T - keep:], v[:, T - keep:]


def band_attn_sample(h, cache_k, cache_v, w_qkv, rel_table, w_out):
    B, T, _ = h.shape
    q, k, v = qkv_heads(h, w_qkv)
    P = cache_k.shape[1]
    kk = jnp.concatenate([cache_k.astype(k.dtype), k], axis=1)
    vv = jnp.concatenate([cache_v.astype(v.dtype), v], axis=1)
    bias = rel_bias(rel_table, P + jnp.arange(T)[:, None] - jnp.arange(P + T)[None, :])
    o = attend(q, kk, vv, bias, True)
    return o.reshape(B, T, D_MODEL) @ w_out, k, v


def linear_recurrence(a, u, h0):
    u = u.at[:, 0].add(a[:, 0] * h0)

    def combine(left, right):
        al, ul = left
        ar, ur = right
        return al * ar, ar * ul + ur

    _, hs = lax.associative_scan(combine, (a, u), axis=1)
    return hs


def rglru_mixer(h, h0, conv_buf, w_in, conv_w, conv_b, w_rgate, b_rgate, w_igate, b_igate, lam, w_out):
    B, T, _ = h.shape
    gate, xr = jnp.split(h @ w_in, 2, axis=-1)
    xpad = jnp.concatenate([conv_buf.astype(xr.dtype), xr], axis=1)
    xc = lax.conv_general_dilated(xpad, conv_w[:, None, :], window_strides=(1,), padding='VALID',
                                  dimension_numbers=('NWC', 'WIO', 'NWC'),
                                  feature_group_count=LRU_WIDTH) + conv_b
    new_buf = xpad[:, T:]
    xb = xc.reshape(B, T, LRU_BLOCKS, LRU_BLOCK)
    r = jax.nn.sigmoid((jnp.einsum('btni,nij->btnj', xb, w_rgate).reshape(B, T, LRU_WIDTH)
                        + b_rgate).astype(jnp.float32))
    i = jax.nn.sigmoid((jnp.einsum('btni,nij->btnj', xb, w_igate).reshape(B, T, LRU_WIDTH)
                        + b_igate).astype(jnp.float32))
    log_a = LRU_C * r * jax.nn.log_sigmoid(lam.astype(jnp.float32))
    a = jnp.exp(log_a)
    u = jnp.sqrt(-jnp.expm1(2.0 * log_a)) * (i * xc.astype(jnp.float32))
    hs = linear_recurrence(a, u, h0.astype(jnp.float32))
    y = (hs.astype(h.dtype) * jax.nn.gelu(gate)) @ w_out
    return y, hs[:, -1].astype(h.dtype), new_buf


def setup_inputs(seed: int = 0) -> dict:
    key = jax.random.key(seed)
    keys = iter(jax.random.split(key, 80))

    def nrm(shape, scale):
        return jax.random.normal(next(keys), shape, jnp.float32) * scale

    def gain(n):
        return 1.0 + nrm((n,), 0.02)

    def ffn(p):
        return {p + 'norm_ffn': gain(D_MODEL),
                p + 'ffn_w_in': nrm((D_MODEL, 2 * D_FF), D_MODEL ** -0.5),
                p + 'ffn_w_out': nrm((D_FF, D_MODEL), D_FF ** -0.5)}

    def gla(p):
        return {p + 'norm_mix': gain(D_MODEL),
                p + 'gla_w_in': nrm((D_MODEL, GLA_IN), D_MODEL ** -0.5),
                p + 'gla_w_gate_a': nrm((D_MODEL, GLA_GATE_RANK), D_MODEL ** -0.5),
                p + 'gla_w_gate_b': nrm((GLA_GATE_RANK, GLA_HEADS * GLA_DK), GLA_GATE_RANK ** -0.5),
                p + 'gla_b_gate': nrm((GLA_HEADS * GLA_DK,), 0.1),
                p + 'gla_head_norm': gain(GLA_DV),
                p + 'gla_w_out': nrm((GLA_HEADS * GLA_DV, D_MODEL), (GLA_HEADS * GLA_DV) ** -0.5)}

    band_rows = min(ATT_BAND, PAST_LEN)
    inp = {}
    inp['x_prompt'] = nrm((BATCH, SEQ, D_MODEL), 1.0)
    inp['x_sample'] = nrm((DEC_BATCH, DEC_SEQ, D_MODEL), 1.0)
    inp['state_l0_gla'] = nrm((DEC_BATCH, GLA_HEADS, GLA_DK, GLA_DV), 0.5)
    inp['cache_l1_k'] = nrm((DEC_BATCH, band_rows, ATT_HEADS, ATT_DH), 1.0)
    inp['cache_l1_v'] = nrm((DEC_BATCH, band_rows, ATT_HEADS, ATT_DH), 1.0)
    inp['state_l2_h'] = nrm((DEC_BATCH, LRU_WIDTH), 0.5)
    inp['state_l2_conv'] = nrm((DEC_BATCH, CONV_WIDTH - 1, LRU_WIDTH), 1.0)
    inp['state_l3_gla'] = nrm((DEC_BATCH, GLA_HEADS, GLA_DK, GLA_DV), 0.5)
    inp.update(gla('l0_'))
    inp.update(ffn('l0_'))
    inp['l1_norm_mix'] = gain(D_MODEL)
    inp['l1_att_w_qkv'] = nrm((D_MODEL, 3 * D_MODEL), D_MODEL ** -0.5)
    inp['l1_att_rel_bias'] = nrm((ATT_HEADS, 2 * ATT_MAX_REL + 1), 0.2)
    inp['l1_att_w_out'] = nrm((D_MODEL, D_MODEL), D_MODEL ** -0.5)
    inp.update(ffn('l1_'))
    inp['l2_norm_mix'] = gain(D_MODEL)
    inp['l2_lru_w_in'] = nrm((D_MODEL, 2 * LRU_WIDTH), D_MODEL ** -0.5)
    inp['l2_lru_conv_w'] = nrm((CONV_WIDTH, LRU_WIDTH), CONV_WIDTH ** -0.5)
    inp['l2_lru_conv_b'] = nrm((LRU_WIDTH,), 0.02)
    inp['l2_lru_w_rgate'] = nrm((LRU_BLOCKS, LRU_BLOCK, LRU_BLOCK), LRU_BLOCK ** -0.5)
    inp['l2_lru_b_rgate'] = nrm((LRU_WIDTH,), 0.1)
    inp['l2_lru_w_igate'] = nrm((LRU_BLOCKS, LRU_BLOCK, LRU_BLOCK), LRU_BLOCK ** -0.5)
    inp['l2_lru_b_igate'] = nrm((LRU_WIDTH,), 0.1)
    a_c = jax.random.uniform(next(keys), (LRU_WIDTH,), jnp.float32, 0.9, 0.999)
    sig = a_c ** (1.0 / LRU_C)
    inp['l2_lru_lambda'] = jnp.log(sig) - jnp.log1p(-sig)
    inp['l2_lru_w_out'] = nrm((LRU_WIDTH, D_MODEL), LRU_WIDTH ** -0.5)
    inp.update(ffn('l2_'))
    inp.update(gla('l3_'))
    inp.update(ffn('l3_'))
    inp['final_norm'] = gain(D_MODEL)
    return inp


def reference(x_prompt, x_sample, state_l0_gla, cache_l1_k, cache_l1_v, state_l2_h, state_l2_conv, state_l3_gla,
              l0_norm_mix, l0_gla_w_in, l0_gla_w_gate_a, l0_gla_w_gate_b, l0_gla_b_gate, l0_gla_head_norm, l0_gla_w_out,
              l0_norm_ffn, l0_ffn_w_in, l0_ffn_w_out,
              l1_norm_mix, l1_att_w_qkv, l1_att_rel_bias, l1_att_w_out, l1_norm_ffn, l1_ffn_w_in, l1_ffn_w_out,
              l2_norm_mix, l2_lru_w_in, l2_lru_conv_w, l2_lru_conv_b, l2_lru_w_rgate, l2_lru_b_rgate,
              l2_lru_w_igate, l2_lru_b_igate, l2_lru_lambda, l2_lru_w_out, l2_norm_ffn, l2_ffn_w_in, l2_ffn_w_out,
              l3_norm_mix, l3_gla_w_in, l3_gla_w_gate_a, l3_gla_w_gate_b, l3_gla_b_gate, l3_gla_head_norm, l3_gla_w_out,
              l3_norm_ffn, l3_ffn_w_in, l3_ffn_w_out,
              final_norm):
    mix_norms = (l0_norm_mix, l1_norm_mix, l2_norm_mix, l3_norm_mix)
    mix_params = (
        (l0_gla_w_in, l0_gla_w_gate_a, l0_gla_w_gate_b, l0_gla_b_gate, l0_gla_head_norm, l0_gla_w_out),
        (l1_att_w_qkv, l1_att_rel_bias, l1_att_w_out),
        (l2_lru_w_in, l2_lru_conv_w, l2_lru_conv_b, l2_lru_w_rgate, l2_lru_b_rgate,
         l2_lru_w_igate, l2_lru_b_igate, l2_lru_lambda, l2_lru_w_out),
        (l3_gla_w_in, l3_gla_w_gate_a, l3_gla_w_gate_b, l3_gla_b_gate, l3_gla_head_norm, l3_gla_w_out),
    )
    ffn_norms = (l0_norm_ffn, l1_norm_ffn, l2_norm_ffn, l3_norm_ffn)
    ffn_params = ((l0_ffn_w_in, l0_ffn_w_out), (l1_ffn_w_in, l1_ffn_w_out),
                  (l2_ffn_w_in, l2_ffn_w_out), (l3_ffn_w_in, l3_ffn_w_out))
    sample_states = ((state_l0_gla,), (cache_l1_k, cache_l1_v), (state_l2_h, state_l2_conv), (state_l3_gla,))

    xp, xs = x_prompt, x_sample
    Bp = x_prompt.shape[0]
    new_state = []
    for i in range(DEPTH):
        kind = MIXER_PATTERN[i % len(MIXER_PATTERN)]
        hp = rmsnorm(xp, mix_norms[i])
        hs = rmsnorm(xs, mix_norms[i])
        if kind == 'gla':
            s0 = jnp.zeros((Bp, GLA_HEADS, GLA_DK, GLA_DV), xp.dtype)
            yp, sp = gla_mixer(hp, s0, *mix_params[i])
            ys, ss = gla_mixer(hs, *sample_states[i], *mix_params[i])
            new_state.append((sp, ss))
        elif kind == 'band_attn':
            yp, kp_, vp_ = band_attn_prompt(hp, *mix_params[i])
            ys, ks_, vs_ = band_attn_sample(hs, *sample_states[i], *mix_params[i])
            new_state.append((kp_, vp_, ks_, vs_))
        else:
            h0 = jnp.zeros((Bp, LRU_WIDTH), xp.dtype)
            buf0 = jnp.zeros((Bp, CONV_WIDTH - 1, LRU_WIDTH), xp.dtype)
            yp, hlp, bufp = rglru_mixer(hp, h0, buf0, *mix_params[i])
            ys, hls, bufs = rglru_mixer(hs, *sample_states[i], *mix_params[i])
            new_state.append((hlp, bufp, hls, bufs))
        xp = xp + yp
        xs = xs + ys
        xp = xp + swiglu_ffn(rmsnorm(xp, ffn_norms[i]), *ffn_params[i])
        xs = xs + swiglu_ffn(rmsnorm(xs, ffn_norms[i]), *ffn_params[i])

    y_prompt = rmsnorm(xp, final_norm)
    y_sample = rmsnorm(xs, final_norm)
    (gla0_p, gla0_s), (k1_p, v1_p, k1_s, v1_s), (h2_p, conv2_p, h2_s, conv2_s), (gla3_p, gla3_s) = new_state
    return (y_prompt, y_sample, gla0_p, gla0_s, k1_p, v1_p, k1_s, v1_s,
            h2_p, conv2_p, h2_s, conv2_s, gla3_p, gla3_s)
```

```python
import functools
import math

import numpy as np
import jax
import jax.numpy as jnp
from jax import lax
from jax.experimental import pallas as pl
from jax.experimental.pallas import tpu as pltpu

F32 = jnp.float32
BF = jnp.bfloat16

D_MODEL = 1024
CHUNK = 64
EPS = 1e-6
NEG_INF = -1e30

GLA_HEADS = 4
GLA_DK = 128
GLA_DV = 256
GLA_RANK = 16
GLA_TAU = 16.0
GLA_HK = GLA_HEADS * GLA_DK
GLA_HV = GLA_HEADS * GLA_DV

ATT_HEADS = 16
ATT_DH = 64
ATT_BAND = 512
ATT_MAX_REL = 128
ATT_QT = 128
ATT_KW = ATT_BAND + ATT_QT
ATT_BLK = 512
ATT_TBL = 384

LRU_W = 1536
LRU_BLOCK = 96
LRU_GROUP = 384
LRU_C = 8.0
CONV_W = 4

D_FF = 2816

LANE = 128
TM_PROJ = 512
TM_POST = 256
VMEM_LIMIT_BYTES = 56 * 1024 * 1024


def _cparams(*sem):
    return pltpu.CompilerParams(dimension_semantics=sem, vmem_limit_bytes=VMEM_LIMIT_BYTES)


def _resident(shape):
    nd = len(shape)
    return pl.BlockSpec(shape, lambda *_: (0,) * nd, pipeline_mode=pl.Buffered(1))


def _dot(a, b):
    return jnp.dot(a, b, preferred_element_type=F32)


def _dot_nt(a, b):
    return lax.dot_general(a, b, (((1,), (1,)), ((), ())), preferred_element_type=F32)


def _dot_tn(a, b):
    return lax.dot_general(a, b, (((0,), (0,)), ((), ())), preferred_element_type=F32)


def _rms(x, g):
    return x * lax.rsqrt(jnp.mean(x * x, axis=-1, keepdims=True) + EPS) * g


def _log_sigmoid(z):
    return jnp.minimum(z, 0.0) - jnp.log1p(jnp.exp(-jnp.abs(z)))


def _silu(x):
    return x * jax.nn.sigmoid(x)


def _gelu_tanh(x):
    return x * (0.5 * (1.0 + jnp.tanh(math.sqrt(2.0 / math.pi) * (x + 0.044715 * (x * x * x)))))


def _row_tile(m, pref):
    t = min(m, pref)
    assert m % t == 0
    return t


def _gla_proj_kernel(x_ref, gn_ref, w_ref, wa_ref, wb_ref, bg_ref, q_ref, k_ref, v_ref, r_ref, g_ref):
    h = _rms(x_ref[...], gn_ref[...]).astype(BF)
    q_ref[...] = _dot(h, w_ref[:, 0:GLA_HK]).astype(BF)
    k_ref[...] = _dot(h, w_ref[:, GLA_HK:2 * GLA_HK]).astype(BF)
    v_ref[...] = _dot(h, w_ref[:, 2 * GLA_HK:2 * GLA_HK + GLA_HV]).astype(BF)
    r_ref[...] = _dot(h, w_ref[:, 2 * GLA_HK + GLA_HV:]).astype(BF)
    low = _dot(h, wa_ref[...]).astype(BF)
    z = _dot(low, wb_ref[...]) + bg_ref[...]
    g_ref[...] = _log_sigmoid(z) * (1.0 / GLA_TAU)


def _gla_proj(x2, gn, w_in, wa, wb, bg, tm):
    m = x2.shape[0]
    row = lambda n: pl.BlockSpec((tm, n), lambda i: (i, 0))
    return pl.pallas_call(
        _gla_proj_kernel,
        grid=(m // tm,),
        in_specs=[row(D_MODEL), _resident(gn.shape), _resident(w_in.shape), _resident(wa.shape),
                  _resident(wb.shape), _resident(bg.shape)],
        out_specs=[row(GLA_HK), row(GLA_HK), row(GLA_HV), row(GLA_HV), row(GLA_HK)],
        out_shape=[jax.ShapeDtypeStruct((m, GLA_HK), BF), jax.ShapeDtypeStruct((m, GLA_HK), BF),
                   jax.ShapeDtypeStruct((m, GLA_HV), BF), jax.ShapeDtypeStruct((m, GLA_HV), BF),
                   jax.ShapeDtypeStruct((m, GLA_HK), F32)],
        compiler_params=_cparams("parallel"),
        name="gla_proj",
    )(x2, gn, w_in, wa, wb, bg)


def _split_proj_kernel(x_ref, gn_ref, w_ref, *out_refs):
    h = _rms(x_ref[...], gn_ref[...]).astype(BF)
    col = 0
    for o_ref in out_refs:
        n = o_ref.shape[1]
        o_ref[...] = _dot(h, w_ref[:, col:col + n]).astype(o_ref.dtype)
        col += n


def _split_proj(x2, gn, w, widths, dtypes, tm, name):
    m = x2.shape[0]
    row = lambda n: pl.BlockSpec((tm, n), lambda i: (i, 0))
    return pl.pallas_call(
        _split_proj_kernel,
        grid=(m // tm,),
        in_specs=[row(D_MODEL), _resident(gn.shape), _resident(w.shape)],
        out_specs=[row(n) for n in widths],
        out_shape=[jax.ShapeDtypeStruct((m, n), dt) for n, dt in zip(widths, dtypes)],
        compiler_params=_cparams("parallel"),
        name=name,
    )(x2, gn, w)


def _post_kernel(*refs, final):
    if final:
        a_ref, x_ref, wo_ref, gn_ref, w1_ref, w2_ref, fn_ref, out_ref = refs
    else:
        a_ref, x_ref, wo_ref, gn_ref, w1_ref, w2_ref, out_ref = refs
    x1 = x_ref[...] + _dot(a_ref[...], wo_ref[...])
    h = _rms(x1, gn_ref[...]).astype(BF)
    g = _dot(h, w1_ref[:, 0:D_FF])
    u = _dot(h, w1_ref[:, D_FF:])
    act = (_silu(g) * u).astype(BF)
    x2 = x1 + _dot(act, w2_ref[...])
    if final:
        x2 = _rms(x2, fn_ref[...])
    out_ref[...] = x2


def _post(a2, x2, wo, gn, w1, w2, fn, tm):
    m = x2.shape[0]
    ka = a2.shape[1]
    final = fn is not None
    row = lambda n: pl.BlockSpec((tm, n), lambda i: (i, 0))
    ins = [a2, x2, wo, gn, w1, w2] + ([fn] if final else [])
    specs = [row(ka), row(D_MODEL)] + [_resident(t.shape) for t in ins[2:]]
    return pl.pallas_call(
        functools.partial(_post_kernel, final=final),
        grid=(m // tm,),
        in_specs=specs,
        out_specs=row(D_MODEL),
        out_shape=jax.ShapeDtypeStruct((m, D_MODEL), F32),
        compiler_params=_cparams("parallel"),
        name="post_final" if final else "post",
    )(*ins)


def _gla_kernel(q_ref, k_ref, v_ref, r_ref, g_ref, s0_ref, hn_ref, o_ref, sfin_ref, s_scr, *, chunk, n_chunks):
    i = pl.program_id(1)

    @pl.when(i == 0)
    def _():
        s_scr[...] = s0_ref[0]

    c_rows = lax.broadcasted_iota(jnp.int32, (chunk, chunk), 0)
    c_cols = lax.broadcasted_iota(jnp.int32, (chunk, chunk), 1)
    causal = c_rows >= c_cols
    tril = jnp.where(causal, 1.0, 0.0).astype(BF)
    hn = hn_ref[...]

    for c in range(n_chunks):
        rs = slice(c * chunk, (c + 1) * chunk)
        g = g_ref[rs, :]
        g_hi = g.astype(BF)
        g_lo = (g - g_hi.astype(F32)).astype(BF)
        b = _dot(tril, g_hi) + _dot(tril, g_lo)
        b_last = b[chunk - 1:chunk, :]
        k = k_ref[rs, :].astype(F32)
        q_e = (q_ref[rs, :].astype(F32) * (GLA_DK ** -0.5)) * jnp.exp(b)
        k_e = k * jnp.exp(-b)
        k_s = k * jnp.exp(b_last - b)
        e_last = jnp.exp(b_last)
        for hd in range(GLA_HEADS):
            ks = slice(hd * GLA_DK, (hd + 1) * GLA_DK)
            vs = slice(hd * GLA_DV, (hd + 1) * GLA_DV)
            qh = q_e[:, ks].astype(BF)
            a = jnp.where(causal, _dot_nt(qh, k_e[:, ks].astype(BF)), 0.0).astype(BF)
            vh = v_ref[rs, vs]
            s_old = s_scr[hd]
            o = _dot(a, vh) + _dot(qh, s_old.astype(BF))
            decay = jnp.transpose(jnp.broadcast_to(e_last[:, ks], (GLA_DK, GLA_DK)))
            upd = _dot_tn(k_s[:, ks].astype(BF), vh)
            s_scr[hd] = jnp.concatenate([decay, decay], axis=1) * s_old + upd
            o = o * lax.rsqrt(jnp.mean(o * o, axis=-1, keepdims=True) + EPS) * hn
            o_ref[rs, vs] = (o * _silu(r_ref[rs, vs].astype(F32))).astype(BF)

    @pl.when(i == pl.num_programs(1) - 1)
    def _():
        sfin_ref[0] = s_scr[...]


def _gla_mix(q, k, v, r, g, s0, hn, batch, seq):
    chunk = min(seq, CHUNK)
    tt = min(seq, 4 * chunk)
    nt = seq // tt
    assert seq % tt == 0 and tt % chunk == 0
    row = lambda n: pl.BlockSpec((tt, n), lambda b, i: (b * nt + i, 0))
    st = pl.BlockSpec((1, GLA_HEADS, GLA_DK, GLA_DV), lambda b, i: (b, 0, 0, 0))
    return pl.pallas_call(
        functools.partial(_gla_kernel, chunk=chunk, n_chunks=tt // chunk),
        grid=(batch, nt),
        in_specs=[row(GLA_HK), row(GLA_HK), row(GLA_HV), row(GLA_HV), row(GLA_HK), st,
                  pl.BlockSpec(hn.shape, lambda b, i: (0, 0))],
        out_specs=[row(GLA_HV), st],
        out_shape=[jax.ShapeDtypeStruct((batch * seq, GLA_HV), BF),
                   jax.ShapeDtypeStruct((batch, GLA_HEADS, GLA_DK, GLA_DV), F32)],
        scratch_shapes=[pltpu.VMEM((GLA_HEADS, GLA_DK, GLA_DV), F32)],
        compiler_params=_cparams("parallel", "arbitrary"),
        name="gla_mix",
    )(q, k, v, r, g, s0, hn)


def _bias_kernel(idx_ref, thi_ref, tmid_ref, tlo_ref, out_ref):
    d = lax.broadcasted_iota(jnp.int32, (ATT_TBL, idx_ref.shape[1]), 0)
    onehot = jnp.where(d == idx_ref[...], 1.0, 0.0).astype(BF)
    out_ref[...] = _dot(thi_ref[...], onehot) + _dot(tmid_ref[...], onehot) + _dot(tlo_ref[...], onehot)


def _rel_bias(table):
    t = np.arange(ATT_QT)[:, None]
    r = np.arange(ATT_KW)[None, :]
    idx = np.clip(ATT_BAND + t - r, -ATT_MAX_REL, ATT_MAX_REL) + ATT_MAX_REL
    idx = jnp.asarray(idx.reshape(1, ATT_QT * ATT_KW), jnp.int32)
    tpad = jnp.pad(table.astype(F32), ((0, 0), (0, ATT_TBL - table.shape[1])))
    t_hi = tpad.astype(BF)
    t_mid = (tpad - t_hi.astype(F32)).astype(BF)
    t_lo = (tpad - t_hi.astype(F32) - t_mid.astype(F32)).astype(BF)
    cols = 2 * ATT_KW
    tab = pl.BlockSpec((ATT_HEADS, ATT_TBL), lambda i: (0, 0))
    out = pl.pallas_call(
        _bias_kernel,
        grid=(ATT_QT * ATT_KW // cols,),
        in_specs=[pl.BlockSpec((1, cols), lambda i: (0, i)), tab, tab, tab],
        out_specs=pl.BlockSpec((ATT_HEADS, cols), lambda i: (0, i)),
        out_shape=jax.ShapeDtypeStruct((ATT_HEADS, ATT_QT * ATT_KW), F32),
        compiler_params=_cparams("parallel"),
        name="rel_bias",
    )(idx, t_hi, t_mid, t_lo)
    return out.reshape(ATT_HEADS, ATT_QT, ATT_KW)


def _attend_tile(q2, kw, vw, bias_ref, hp, valid, nq):
    lane = lax.broadcasted_iota(jnp.int32, (nq, LANE), 1)
    outs = []
    for sub in range(2):
        mine = (lane < ATT_DH) if sub == 0 else (lane >= ATT_DH)
        qh = jnp.where(mine, q2, jnp.zeros_like(q2))
        s = _dot_nt(qh, kw) * (ATT_DH ** -0.5) + bias_ref[2 * hp + sub, 0:nq, :]
        s = jnp.where(valid, s, NEG_INF)
        p = jnp.exp(s - jnp.max(s, axis=-1, keepdims=True))
        inv = 1.0 / jnp.sum(p, axis=-1, keepdims=True)
        outs.append(_dot(p.astype(BF), vw) * inv)
    return jnp.where(lane < ATT_DH, outs[0], outs[1])


def _attn_prompt_kernel(q_ref, kp_ref, kc_ref, vp_ref, vc_ref, bias_ref, o_ref):
    i = pl.program_id(1)
    t = lax.broadcasted_iota(jnp.int32, (ATT_QT, ATT_KW), 0)
    r = lax.broadcasted_iota(jnp.int32, (ATT_QT, ATT_KW), 1)
    band = ((t < CHUNK) & (r < ATT_BAND + CHUNK)) | ((t >= CHUNK) & (r >= CHUNK))
    for jj in range(ATT_BLK // ATT_QT):
        lo = jj * ATT_QT
        valid = band & (r >= (1 - i) * ATT_BLK - lo)
        for hp in range(ATT_HEADS // 2):
            ls = slice(hp * LANE, (hp + 1) * LANE)
            kw = jnp.concatenate([kp_ref[lo:, ls], kc_ref[0:lo + ATT_QT, ls]], axis=0)
            vw = jnp.concatenate([vp_ref[lo:, ls], vc_ref[0:lo + ATT_QT, ls]], axis=0)
            o = _attend_tile(q_ref[lo:lo + ATT_QT, ls], kw, vw, bias_ref, hp, valid, ATT_QT)
            o_ref[lo:lo + ATT_QT, ls] = o.astype(BF)


def _attn_prompt(q, k, v, bias, batch, seq):
    nt = seq // ATT_BLK
    assert seq % ATT_BLK == 0
    cur = pl.BlockSpec((ATT_BLK, D_MODEL), lambda b, i: (b * nt + i, 0))
    prev = pl.BlockSpec((ATT_BLK, D_MODEL), lambda b, i: (b * nt + jnp.maximum(i - 1, 0), 0))
    return pl.pallas_call(
        _attn_prompt_kernel,
        grid=(batch, nt),
        in_specs=[cur, prev, cur, prev, cur, _resident(bias.shape)],
        out_specs=cur,
        out_shape=jax.ShapeDtypeStruct((batch * seq, D_MODEL), BF),
        compiler_params=_cparams("parallel", "arbitrary"),
        name="attn_prompt",
    )(q, k, k, v, v, bias)


def _attn_sample_kernel(q_ref, kw_ref, vw_ref, bias_ref, o_ref, *, nq, n_keys):
    r = lax.broadcasted_iota(jnp.int32, (nq, ATT_KW), 1)
    valid = r < n_keys
    for hp in range(ATT_HEADS // 2):
        ls = slice(hp * LANE, (hp + 1) * LANE)
        o = _attend_tile(q_ref[0, :, ls], kw_ref[0, :, ls], vw_ref[0, :, ls], bias_ref, hp, valid, nq)
        o_ref[0, :, ls] = o.astype(BF)


def _attn_sample(q3, kw3, vw3, bias, n_keys):
    batch, nq, _ = q3.shape
    blk = lambda n: pl.BlockSpec((1, n, D_MODEL), lambda b: (b, 0, 0))
    return pl.pallas_call(
        functools.partial(_attn_sample_kernel, nq=nq, n_keys=n_keys),
        grid=(batch,),
        in_specs=[blk(nq), blk(ATT_KW), blk(ATT_KW), _resident(bias.shape)],
        out_specs=blk(nq),
        out_shape=jax.ShapeDtypeStruct((batch, nq, D_MODEL), BF),
        compiler_params=_cparams("parallel"),
        name="attn_sample",
    )(q3, kw3, vw3, bias)


def _lru_kernel(xr_ref, gate_ref, h0_ref, cb_ref, cw_ref, cbias_ref, wr_ref, br_ref, wi_ref, bi_ref, lam_ref,
                y_ref, hfin_ref, cfin_ref, xs_scr, h_scr, *, tt):
    i = pl.program_id(1)

    @pl.when(i == 0)
    def _():
        xs_scr[0:8, :] = jnp.zeros((8, LRU_W), F32)
        xs_scr[8 - (CONV_W - 1):8, :] = cb_ref[0]
        h_scr[...] = h0_ref[0]

    xs_scr[8:, :] = xr_ref[...]
    cw = cw_ref[...]
    xc = cbias_ref[...] + cw[CONV_W - 1:CONV_W, :] * xs_scr[8:, :]
    for j in range(CONV_W - 1):
        xc = xc + cw[j:j + 1, :] * xs_scr[pl.ds(8 - (CONV_W - 1) + j, tt), :]
    xcb = xc.astype(BF)
    rg, ig = [], []
    for gi in range(LRU_W // LRU_GROUP):
        gs = slice(gi * LRU_GROUP, (gi + 1) * LRU_GROUP)
        rg.append(_dot(xcb[:, gs], wr_ref[gi]))
        ig.append(_dot(xcb[:, gs], wi_ref[gi]))
    rgate = jax.nn.sigmoid(jnp.concatenate(rg, axis=1) + br_ref[...])
    igate = jax.nn.sigmoid(jnp.concatenate(ig, axis=1) + bi_ref[...])
    log_a = LRU_C * rgate * _log_sigmoid(lam_ref[...])
    a = jnp.exp(log_a)
    u = jnp.sqrt(-jnp.tanh(log_a) * (a * a + 1.0)) * (igate * xc)

    rows = lax.broadcasted_iota(jnp.int32, (tt, LRU_W), 0)
    step = 1
    while step < tt:
        keep = rows >= step
        a_sh = jnp.where(keep, pltpu.roll(a, step, 0), 1.0)
        u_sh = jnp.where(keep, pltpu.roll(u, step, 0), 0.0)
        u = a * u_sh + u
        a = a * a_sh
        step *= 2
    hs = a * h_scr[...] + u
    h_scr[...] = hs[tt - 1:tt, :]
    y_ref[...] = (hs * _gelu_tanh(gate_ref[...].astype(F32))).astype(BF)
    xs_scr[0:8, :] = xs_scr[tt:tt + 8, :]

    @pl.when(i == pl.num_programs(1) - 1)
    def _():
        hfin_ref[0] = hs[tt - 1:tt, :]
        cfin_ref[0] = xs_scr[tt + 8 - (CONV_W - 1):tt + 8, :]


def _lru_mix(xr, gate, h0, cbuf, cw, cbias, wr, br, wi, bi, lam, batch, seq):
    tt = min(seq, 256)
    nt = seq // tt
    assert seq % tt == 0 and tt >= 8
    row = lambda n: pl.BlockSpec((tt, n), lambda b, i: (b * nt + i, 0))
    per_b = lambda n: pl.BlockSpec((1, n, LRU_W), lambda b, i: (b, 0, 0))
    const = lambda t: pl.BlockSpec(t.shape, lambda b, i: (0,) * t.ndim)
    return pl.pallas_call(
        functools.partial(_lru_kernel, tt=tt),
        grid=(batch, nt),
        in_specs=[row(LRU_W), row(LRU_W), per_b(1), per_b(CONV_W - 1), const(cw), const(cbias),
                  const(wr), const(br), const(wi), const(bi), const(lam)],
        out_specs=[row(LRU_W), per_b(1), per_b(CONV_W - 1)],
        out_shape=[jax.ShapeDtypeStruct((batch * seq, LRU_W), BF),
                   jax.ShapeDtypeStruct((batch, 1, LRU_W), F32),
                   jax.ShapeDtypeStruct((batch, CONV_W - 1, LRU_W), F32)],
        scratch_shapes=[pltpu.VMEM((tt + 8, LRU_W), F32), pltpu.VMEM((1, LRU_W), F32)],
        compiler_params=_cparams("parallel", "arbitrary"),
        name="lru_mix",
    )(xr, gate, h0, cbuf, cw, cbias, wr, br, wi, bi, lam)


def _group_block_diag(w):
    per = LRU_GROUP // LRU_BLOCK
    wg = w.reshape(LRU_W // LRU_GROUP, per, LRU_BLOCK, LRU_BLOCK)
    eye = jnp.eye(per, dtype=w.dtype)
    out = jnp.einsum('gpij,pq->gpiqj', wg, eye)
    return out.reshape(LRU_W // LRU_GROUP, LRU_GROUP, LRU_GROUP).astype(BF)


def _row(v):
    return v.reshape(1, -1).astype(F32)


def kernel(x_prompt, x_sample, state_l0_gla, cache_l1_k, cache_l1_v, state_l2_h, state_l2_conv, state_l3_gla, l0_norm_mix, l0_gla_w_in, l0_gla_w_gate_a, l0_gla_w_gate_b, l0_gla_b_gate, l0_gla_head_norm, l0_gla_w_out, l0_norm_ffn, l0_ffn_w_in, l0_ffn_w_out, l1_norm_mix, l1_att_w_qkv, l1_att_rel_bias, l1_att_w_out, l1_norm_ffn, l1_ffn_w_in, l1_ffn_w_out, l2_norm_mix, l2_lru_w_in, l2_lru_conv_w, l2_lru_conv_b, l2_lru_w_rgate, l2_lru_b_rgate, l2_lru_w_igate, l2_lru_b_igate, l2_lru_lambda, l2_lru_w_out, l2_norm_ffn, l2_ffn_w_in, l2_ffn_w_out, l3_norm_mix, l3_gla_w_in, l3_gla_w_gate_a, l3_gla_w_gate_b, l3_gla_b_gate, l3_gla_head_norm, l3_gla_w_out, l3_norm_ffn, l3_ffn_w_in, l3_ffn_w_out, final_norm):
    bp, tp, _ = x_prompt.shape
    bs, ts, _ = x_sample.shape
    streams = [(x_prompt.reshape(bp * tp, D_MODEL), bp, tp), (x_sample.reshape(bs * ts, D_MODEL), bs, ts)]
    tm_proj = TM_PROJ
    tm_post = TM_POST

    def post(xs, mixed, wo, gn, w1, w2, fn=None):
        wo_b, w1_b, w2_b = wo.astype(BF), w1.astype(BF), w2.astype(BF)
        fn_r = None if fn is None else _row(fn)
        return [_post(a, x, wo_b, _row(gn), w1_b, w2_b, fn_r, _row_tile(x.shape[0], tm_post))
                for a, x in zip(mixed, xs)]

    def gla_layer(xs, states, norm_mix, w_in, wa, wb, bg, hn, w_out, norm_ffn, f_in, f_out, fn=None):
        wa_p = jnp.pad(wa, ((0, 0), (0, LANE - GLA_RANK))).astype(BF)
        wb_p = jnp.pad(wb, ((0, LANE - GLA_RANK), (0, 0))).astype(BF)
        w_in_b = w_in.astype(BF)
        mixed, new_states = [], []
        for (x, b, t), s0 in zip(xs, states):
            q, k, v, r, g = _gla_proj(x, _row(norm_mix), w_in_b, wa_p, wb_p, _row(bg), _row_tile(x.shape[0], tm_proj))
            o, s_new = _gla_mix(q, k, v, r, g, s0.astype(F32), _row(hn), b, t)
            mixed.append(o)
            new_states.append(s_new)
        outs = post([x for x, _, _ in xs], mixed, w_out, norm_ffn, f_in, f_out, fn)
        return [(o, b, t) for o, (_, b, t) in zip(outs, xs)], new_states

    zeros_gla = jnp.zeros((bp, GLA_HEADS, GLA_DK, GLA_DV), F32)
    streams, (gla0_p, gla0_s) = gla_layer(
        streams, [zeros_gla, state_l0_gla], l0_norm_mix, l0_gla_w_in, l0_gla_w_gate_a, l0_gla_w_gate_b,
        l0_gla_b_gate, l0_gla_head_norm, l0_gla_w_out, l0_norm_ffn, l0_ffn_w_in, l0_ffn_w_out)

    bias = _rel_bias(l1_att_rel_bias)
    w_qkv = l1_att_w_qkv.astype(BF)
    (xp, _, _), (xs_, _, _) = streams
    qp, kp, vp = _split_proj(xp, _row(l1_norm_mix), w_qkv, [D_MODEL] * 3, [BF] * 3, _row_tile(bp * tp, tm_proj), "qkv_proj")
    qs, ks, vs = _split_proj(xs_, _row(l1_norm_mix), w_qkv, [D_MODEL] * 3, [BF] * 3, _row_tile(bs * ts, tm_proj), "qkv_proj")
    op = _attn_prompt(qp, kp, vp, bias, bp, tp)
    n_cache = cache_l1_k.shape[1]
    n_keys = n_cache + ts
    assert n_cache == ATT_BAND and n_keys <= ATT_KW and ts <= ATT_QT

    def window(cache, new):
        c = cache.reshape(bs, n_cache, D_MODEL).astype(BF)
        z = jnp.zeros((bs, ATT_KW - n_keys, D_MODEL), BF)
        return jnp.concatenate([c, new.reshape(bs, ts, D_MODEL), z], axis=1)

    os_ = _attn_sample(qs.reshape(bs, ts, D_MODEL), window(cache_l1_k, ks), window(cache_l1_v, vs), bias, n_keys)
    keep = min(ATT_BAND, tp)
    k1_p = kp.reshape(bp, tp, ATT_HEADS, ATT_DH)[:, tp - keep:].astype(F32)
    v1_p = vp.reshape(bp, tp, ATT_HEADS, ATT_DH)[:, tp - keep:].astype(F32)
    k1_s = ks.reshape(bs, ts, ATT_HEADS, ATT_DH).astype(F32)
    v1_s = vs.reshape(bs, ts, ATT_HEADS, ATT_DH).astype(F32)
    outs = post([xp, xs_], [op, os_.reshape(bs * ts, D_MODEL)], l1_att_w_out, l1_norm_ffn, l1_ffn_w_in, l1_ffn_w_out)
    streams = [(outs[0], bp, tp), (outs[1], bs, ts)]

    w_lru = l2_lru_w_in.astype(BF)
    wr = _group_block_diag(l2_lru_w_rgate)
    wi = _group_block_diag(l2_lru_w_igate)
    lru_states = [(jnp.zeros((bp, 1, LRU_W), F32), jnp.zeros((bp, CONV_W - 1, LRU_W), F32)),
                  (state_l2_h.reshape(bs, 1, LRU_W).astype(F32), state_l2_conv.astype(F32))]
    mixed, lru_new = [], []
    for (x, b, t), (h0, cbuf) in zip(streams, lru_states):
        gate, xr = _split_proj(x, _row(l2_norm_mix), w_lru, [LRU_W] * 2, [BF, F32], _row_tile(x.shape[0], tm_proj), "lru_proj")
        y, hfin, cfin = _lru_mix(xr, gate, h0, cbuf, l2_lru_conv_w.astype(F32), _row(l2_lru_conv_b), wr,
                                 _row(l2_lru_b_rgate), wi, _row(l2_lru_b_igate), _row(l2_lru_lambda), b, t)
        mixed.append(y)
        lru_new.append((hfin.reshape(b, LRU_W), cfin))
    outs = post([x for x, _, _ in streams], mixed, l2_lru_w_out, l2_norm_ffn, l2_ffn_w_in, l2_ffn_w_out)
    streams = [(outs[0], bp, tp), (outs[1], bs, ts)]
    (h2_p, conv2_p), (h2_s, conv2_s) = lru_new

    streams, (gla3_p, gla3_s) = gla_layer(
        streams, [zeros_gla, state_l3_gla], l3_norm_mix, l3_gla_w_in, l3_gla_w_gate_a, l3_gla_w_gate_b,
        l3_gla_b_gate, l3_gla_head_norm, l3_gla_w_out, l3_norm_ffn, l3_ffn_w_in, l3_ffn_w_out, fn=final_norm)

    y_prompt = streams[0][0].reshape(bp, tp, D_MODEL)
    y_sample = streams[1][0].reshape(bs, ts, D_MODEL)
    return (y_prompt, y_sample, gla0_p, gla0_s, k1_p, v1_p, k1_s, v1_s,
            h2_p, conv2_p, h2_s, conv2_s, gla3_p, gla3_s)
```

```python
import functools
import math

import numpy as np
import jax
import jax.numpy as jnp
from jax import lax
from jax.experimental import pallas as pl
from jax.experimental.pallas import tpu as pltpu

F32 = jnp.float32
BF = jnp.bfloat16

D_MODEL = 1024
CHUNK = 64
EPS = 1e-6
NEG_INF = -1e30
LOG2E = 1.4426950408889634

GLA_HEADS = 4
GLA_DK = 128
GLA_DV = 256
GLA_RANK = 16
GLA_TAU = 16.0
GLA_HK = GLA_HEADS * GLA_DK
GLA_HV = GLA_HEADS * GLA_DV

ATT_HEADS = 16
ATT_DH = 64
ATT_BAND = 512
ATT_MAX_REL = 128
ATT_QT = 128
ATT_KW = ATT_BAND + ATT_QT
ATT_BLK = 512
ATT_TBL = 384
ATT_LOOKAHEAD = 2

LRU_W = 1536
LRU_BLOCK = 96
LRU_GROUP = 384
LRU_C = 8.0
CONV_W = 4

D_FF = 2816

LANE = 128
TM_PROJ = 512
TM_POST = 256
VMEM_LIMIT_BYTES = 56 * 1024 * 1024


def _cparams(*sem):
    return pltpu.CompilerParams(dimension_semantics=sem, vmem_limit_bytes=VMEM_LIMIT_BYTES)


def _resident(shape):
    nd = len(shape)
    return pl.BlockSpec(shape, lambda *_: (0,) * nd, pipeline_mode=pl.Buffered(1))


def _dot(a, b):
    return jnp.dot(a, b, preferred_element_type=F32)


def _dot_nt(a, b):
    return lax.dot_general(a, b, (((1,), (1,)), ((), ())), preferred_element_type=F32)


def _dot_tn(a, b):
    return lax.dot_general(a, b, (((0,), (0,)), ((), ())), preferred_element_type=F32)


def _rms(x, g):
    return x * lax.rsqrt(jnp.mean(x * x, axis=-1, keepdims=True) + EPS) * g


def _log_sigmoid(z):
    return jnp.minimum(z, 0.0) - jnp.log1p(jnp.exp(-jnp.abs(z)))


def _silu(x):
    return x * jax.nn.sigmoid(x)


def _gelu_tanh(x):
    return x * (0.5 * (1.0 + jnp.tanh(math.sqrt(2.0 / math.pi) * (x + 0.044715 * (x * x * x)))))


def _row_tile(m, pref):
    t = min(m, pref)
    assert m % t == 0
    return t


def _gla_proj_kernel(x_ref, gn_ref, w_ref, wa_ref, wb_ref, bg_ref, q_ref, k_ref, v_ref, r_ref, g_ref):
    h = _rms(x_ref[...], gn_ref[...]).astype(BF)
    q_ref[...] = _dot(h, w_ref[:, 0:GLA_HK]).astype(BF)
    k_ref[...] = _dot(h, w_ref[:, GLA_HK:2 * GLA_HK]).astype(BF)
    v_ref[...] = _dot(h, w_ref[:, 2 * GLA_HK:2 * GLA_HK + GLA_HV]).astype(BF)
    r_ref[...] = _dot(h, w_ref[:, 2 * GLA_HK + GLA_HV:]).astype(BF)
    low = _dot(h, wa_ref[...]).astype(BF)
    z = _dot(low, wb_ref[...]) + bg_ref[...]
    g_ref[...] = _log_sigmoid(z) * (1.0 / GLA_TAU)


def _gla_proj(x2, gn, w_in, wa, wb, bg, tm):
    m = x2.shape[0]
    row = lambda n: pl.BlockSpec((tm, n), lambda i: (i, 0))
    return pl.pallas_call(
        _gla_proj_kernel,
        grid=(m // tm,),
        in_specs=[row(D_MODEL), _resident(gn.shape), _resident(w_in.shape), _resident(wa.shape),
                  _resident(wb.shape), _resident(bg.shape)],
        out_specs=[row(GLA_HK), row(GLA_HK), row(GLA_HV), row(GLA_HV), row(GLA_HK)],
        out_shape=[jax.ShapeDtypeStruct((m, GLA_HK), BF), jax.ShapeDtypeStruct((m, GLA_HK), BF),
                   jax.ShapeDtypeStruct((m, GLA_HV), BF), jax.ShapeDtypeStruct((m, GLA_HV), BF),
                   jax.ShapeDtypeStruct((m, GLA_HK), F32)],
        compiler_params=_cparams("parallel"),
        name="gla_proj",
    )(x2, gn, w_in, wa, wb, bg)


def _split_proj_kernel(x_ref, gn_ref, w_ref, *out_refs, scales):
    h = _rms(x_ref[...], gn_ref[...]).astype(BF)
    col = 0
    for o_ref, scale in zip(out_refs, scales):
        n = o_ref.shape[1]
        acc = _dot(h, w_ref[:, col:col + n])
        o_ref[...] = (acc if scale == 1.0 else acc * scale).astype(o_ref.dtype)
        col += n


def _split_proj(x2, gn, w, widths, dtypes, tm, name, scales=None):
    scales = tuple(scales) if scales is not None else (1.0,) * len(widths)
    m = x2.shape[0]
    row = lambda n: pl.BlockSpec((tm, n), lambda i: (i, 0))
    return pl.pallas_call(
        functools.partial(_split_proj_kernel, scales=scales),
        grid=(m // tm,),
        in_specs=[row(D_MODEL), _resident(gn.shape), _resident(w.shape)],
        out_specs=[row(n) for n in widths],
        out_shape=[jax.ShapeDtypeStruct((m, n), dt) for n, dt in zip(widths, dtypes)],
        compiler_params=_cparams("parallel"),
        name=name,
    )(x2, gn, w)


def _post_kernel(*refs, final):
    if final:
        a_ref, x_ref, wo_ref, gn_ref, w1_ref, w2_ref, fn_ref, out_ref = refs
    else:
        a_ref, x_ref, wo_ref, gn_ref, w1_ref, w2_ref, out_ref = refs
    x1 = x_ref[...] + _dot(a_ref[...], wo_ref[...])
    h = _rms(x1, gn_ref[...]).astype(BF)
    g = _dot(h, w1_ref[:, 0:D_FF])
    u = _dot(h, w1_ref[:, D_FF:])
    act = (_silu(g) * u).astype(BF)
    x2 = x1 + _dot(act, w2_ref[...])
    if final:
        x2 = _rms(x2, fn_ref[...])
    out_ref[...] = x2


def _post(a2, x2, wo, gn, w1, w2, fn, tm):
    m = x2.shape[0]
    ka = a2.shape[1]
    final = fn is not None
    row = lambda n: pl.BlockSpec((tm, n), lambda i: (i, 0))
    ins = [a2, x2, wo, gn, w1, w2] + ([fn] if final else [])
    specs = [row(ka), row(D_MODEL)] + [_resident(t.shape) for t in ins[2:]]
    return pl.pallas_call(
        functools.partial(_post_kernel, final=final),
        grid=(m // tm,),
        in_specs=specs,
        out_specs=row(D_MODEL),
        out_shape=jax.ShapeDtypeStruct((m, D_MODEL), F32),
        compiler_params=_cparams("parallel"),
        name="post_final" if final else "post",
    )(*ins)


def _gla_kernel(q_ref, k_ref, v_ref, r_ref, g_ref, s0_ref, hn_ref, o_ref, sfin_ref, s_scr, *, chunk, n_chunks):
    i = pl.program_id(1)

    @pl.when(i == 0)
    def _():
        s_scr[...] = s0_ref[0]

    c_rows = lax.broadcasted_iota(jnp.int32, (chunk, chunk), 0)
    c_cols = lax.broadcasted_iota(jnp.int32, (chunk, chunk), 1)
    causal = c_rows >= c_cols
    tril = jnp.where(causal, 1.0, 0.0).astype(BF)
    hn = hn_ref[...]

    for c in range(n_chunks):
        rs = slice(c * chunk, (c + 1) * chunk)
        g = g_ref[rs, :]
        g_hi = g.astype(BF)
        g_lo = (g - g_hi.astype(F32)).astype(BF)
        b = _dot(tril, g_hi) + _dot(tril, g_lo)
        b_last = b[chunk - 1:chunk, :]
        k = k_ref[rs, :].astype(F32)
        q_e = (q_ref[rs, :].astype(F32) * (GLA_DK ** -0.5)) * jnp.exp(b)
        k_e = k * jnp.exp(-b)
        k_s = k * jnp.exp(b_last - b)
        e_last = jnp.exp(b_last)
        for hd in range(GLA_HEADS):
            ks = slice(hd * GLA_DK, (hd + 1) * GLA_DK)
            vs = slice(hd * GLA_DV, (hd + 1) * GLA_DV)
            qh = q_e[:, ks].astype(BF)
            a = jnp.where(causal, _dot_nt(qh, k_e[:, ks].astype(BF)), 0.0).astype(BF)
            vh = v_ref[rs, vs]
            s_old = s_scr[hd]
            o = _dot(a, vh) + _dot(qh, s_old.astype(BF))
            decay = jnp.transpose(jnp.broadcast_to(e_last[:, ks], (GLA_DK, GLA_DK)))
            upd = _dot_tn(k_s[:, ks].astype(BF), vh)
            s_scr[hd] = jnp.concatenate([decay, decay], axis=1) * s_old + upd
            o = o * lax.rsqrt(jnp.mean(o * o, axis=-1, keepdims=True) + EPS) * hn
            o_ref[rs, vs] = (o * _silu(r_ref[rs, vs].astype(F32))).astype(BF)

    @pl.when(i == pl.num_programs(1) - 1)
    def _():
        sfin_ref[0] = s_scr[...]


def _gla_mix(q, k, v, r, g, s0, hn, batch, seq):
    chunk = min(seq, CHUNK)
    tt = min(seq, 4 * chunk)
    nt = seq // tt
    assert seq % tt == 0 and tt % chunk == 0
    row = lambda n: pl.BlockSpec((tt, n), lambda b, i: (b * nt + i, 0))
    st = pl.BlockSpec((1, GLA_HEADS, GLA_DK, GLA_DV), lambda b, i: (b, 0, 0, 0))
    return pl.pallas_call(
        functools.partial(_gla_kernel, chunk=chunk, n_chunks=tt // chunk),
        grid=(batch, nt),
        in_specs=[row(GLA_HK), row(GLA_HK), row(GLA_HV), row(GLA_HV), row(GLA_HK), st,
                  pl.BlockSpec(hn.shape, lambda b, i: (0, 0))],
        out_specs=[row(GLA_HV), st],
        out_shape=[jax.ShapeDtypeStruct((batch * seq, GLA_HV), BF),
                   jax.ShapeDtypeStruct((batch, GLA_HEADS, GLA_DK, GLA_DV), F32)],
        scratch_shapes=[pltpu.VMEM((GLA_HEADS, GLA_DK, GLA_DV), F32)],
        compiler_params=_cparams("parallel", "arbitrary"),
        name="gla_mix",
    )(q, k, v, r, g, s0, hn)


def _bias_kernel(idx_ref, madd_ref, thi_ref, tmid_ref, tlo_ref, out_ref):
    d = lax.broadcasted_iota(jnp.int32, (ATT_TBL, idx_ref.shape[1]), 0)
    onehot = jnp.where(d == idx_ref[...], 1.0, 0.0).astype(BF)
    b = _dot(thi_ref[...], onehot) + _dot(tmid_ref[...], onehot) + _dot(tlo_ref[...], onehot)
    out_ref[...] = b * LOG2E + madd_ref[...]


def _rel_bias(table, nq, visible):
    t = np.arange(nq)[:, None]
    r = np.arange(ATT_KW)[None, :]
    idx = np.clip(ATT_BAND + t - r, -ATT_MAX_REL, ATT_MAX_REL) + ATT_MAX_REL
    idx = jnp.asarray(idx.reshape(1, nq * ATT_KW), jnp.int32)
    madd = jnp.asarray(np.where(visible(t, r), 0.0, NEG_INF).reshape(1, nq * ATT_KW), F32)
    tpad = jnp.pad(table.astype(F32), ((0, 0), (0, ATT_TBL - table.shape[1])))
    t_hi = tpad.astype(BF)
    t_mid = (tpad - t_hi.astype(F32)).astype(BF)
    t_lo = (tpad - t_hi.astype(F32) - t_mid.astype(F32)).astype(BF)
    cols = 2 * ATT_KW
    tab = pl.BlockSpec((ATT_HEADS, ATT_TBL), lambda i: (0, 0))
    vec = pl.BlockSpec((1, cols), lambda i: (0, i))
    out = pl.pallas_call(
        _bias_kernel,
        grid=(nq * ATT_KW // cols,),
        in_specs=[vec, vec, tab, tab, tab],
        out_specs=pl.BlockSpec((ATT_HEADS, cols), lambda i: (0, i)),
        out_shape=jax.ShapeDtypeStruct((ATT_HEADS, nq * ATT_KW), F32),
        compiler_params=_cparams("parallel"),
        name="rel_bias",
    )(idx, madd, t_hi, t_mid, t_lo)
    return out.reshape(ATT_HEADS, nq, ATT_KW)


def _att_scores(q2, kw, sub):
    lane = lax.broadcasted_iota(jnp.int32, q2.shape, 1)
    mine = (lane < ATT_DH) if sub == 0 else (lane >= ATT_DH)
    return _dot_nt(jnp.where(mine, q2, jnp.zeros_like(q2)), kw)


def _att_probs(s, bias2, visible):
    s2 = s * LOG2E + bias2
    if visible is not None:
        s2 = jnp.where(visible, s2, NEG_INF)
    p = jnp.exp2(s2 - jnp.max(s2, axis=-1, keepdims=True))
    return p.astype(BF), 1.0 / jnp.sum(p, axis=-1, keepdims=True)


def _att_pipeline(items, scores, probs, finish):
    n = len(items)
    queue = [scores(items[j]) for j in range(min(ATT_LOOKAHEAD, n))]
    pending = None
    for j, item in enumerate(items):
        if j + ATT_LOOKAHEAD < n:
            queue.append(scores(items[j + ATT_LOOKAHEAD]))
        p_inv = probs(item, queue.pop(0))
        if pending is not None:
            finish(*pending)
        pending = (item, p_inv)
    finish(*pending)


def _attn_prompt_kernel(q_ref, kp_ref, kc_ref, vp_ref, vc_ref, bias_ref, o_ref):
    items = [(jj, hp, sub) for jj in range(ATT_BLK // ATT_QT) for hp in range(ATT_HEADS // 2) for sub in range(2)]
    r = lax.broadcasted_iota(jnp.int32, (ATT_QT, ATT_KW), 1)
    lane = lax.broadcasted_iota(jnp.int32, (ATT_QT, LANE), 1)

    def window(prev_ref, cur_ref, jj, hp):
        lo, ls = jj * ATT_QT, slice(hp * LANE, (hp + 1) * LANE)
        return jnp.concatenate([prev_ref[lo:, ls], cur_ref[0:lo + ATT_QT, ls]], axis=0)

    def scores(item):
        jj, hp, sub = item
        return _att_scores(q_ref[jj * ATT_QT:(jj + 1) * ATT_QT, hp * LANE:(hp + 1) * LANE],
                           window(kp_ref, kc_ref, jj, hp), sub)

    def body(first_block):
        held = {}

        def probs(item, s):
            jj, hp, sub = item
            visible = (r >= ATT_BLK - jj * ATT_QT) if first_block else None
            return _att_probs(s, bias_ref[2 * hp + sub], visible)

        def finish(item, p_inv):
            jj, hp, sub = item
            o = _dot(p_inv[0], window(vp_ref, vc_ref, jj, hp)) * p_inv[1]
            if sub == 0:
                held[0] = o
            else:
                o_ref[jj * ATT_QT:(jj + 1) * ATT_QT, hp * LANE:(hp + 1) * LANE] = (
                    jnp.where(lane < ATT_DH, held[0], o).astype(BF))

        _att_pipeline(items, scores, probs, finish)

    i = pl.program_id(1)
    pl.when(i == 0)(lambda: body(True))
    pl.when(i > 0)(lambda: body(False))


def _attn_prompt(q, k, v, bias, batch, seq):
    nt = seq // ATT_BLK
    assert seq % ATT_BLK == 0
    cur = pl.BlockSpec((ATT_BLK, D_MODEL), lambda b, i: (b * nt + i, 0))
    prev = pl.BlockSpec((ATT_BLK, D_MODEL), lambda b, i: (b * nt + jnp.maximum(i - 1, 0), 0))
    return pl.pallas_call(
        _attn_prompt_kernel,
        grid=(batch, nt),
        in_specs=[cur, prev, cur, prev, cur, _resident(bias.shape)],
        out_specs=cur,
        out_shape=jax.ShapeDtypeStruct((batch * seq, D_MODEL), BF),
        compiler_params=_cparams("parallel", "arbitrary"),
        name="attn_prompt",
    )(q, k, k, v, v, bias)


def _attn_sample_kernel(q_ref, kw_ref, vw_ref, bias_ref, o_ref):
    nq = q_ref.shape[1]
    items = [(hp, sub) for hp in range(ATT_HEADS // 2) for sub in range(2)]
    lane = lax.broadcasted_iota(jnp.int32, (nq, LANE), 1)
    held = {}

    def scores(item):
        hp, sub = item
        ls = slice(hp * LANE, (hp + 1) * LANE)
        return _att_scores(q_ref[0, :, ls], kw_ref[0, :, ls], sub)

    def probs(item, s):
        hp, sub = item
        return _att_probs(s, bias_ref[2 * hp + sub], None)

    def finish(item, p_inv):
        hp, sub = item
        ls = slice(hp * LANE, (hp + 1) * LANE)
        o = _dot(p_inv[0], vw_ref[0, :, ls]) * p_inv[1]
        if sub == 0:
            held[0] = o
        else:
            o_ref[0, :, ls] = jnp.where(lane < ATT_DH, held[0], o).astype(BF)

    _att_pipeline(items, scores, probs, finish)


def _attn_sample(q3, kw3, vw3, bias):
    batch, nq, _ = q3.shape
    blk = lambda n: pl.BlockSpec((1, n, D_MODEL), lambda b: (b, 0, 0))
    return pl.pallas_call(
        _attn_sample_kernel,
        grid=(batch,),
        in_specs=[blk(nq), blk(ATT_KW), blk(ATT_KW), _resident(bias.shape)],
        out_specs=blk(nq),
        out_shape=jax.ShapeDtypeStruct((batch, nq, D_MODEL), BF),
        compiler_params=_cparams("parallel"),
        name="attn_sample",
    )(q3, kw3, vw3, bias)


def _lru_kernel(xr_ref, gate_ref, h0_ref, cb_ref, cw_ref, cbias_ref, wr_ref, br_ref, wi_ref, bi_ref, lam_ref,
                y_ref, hfin_ref, cfin_ref, xs_scr, h_scr, *, tt):
    i = pl.program_id(1)

    @pl.when(i == 0)
    def _():
        xs_scr[0:8, :] = jnp.zeros((8, LRU_W), F32)
        xs_scr[8 - (CONV_W - 1):8, :] = cb_ref[0]
        h_scr[...] = h0_ref[0]

    xs_scr[8:, :] = xr_ref[...]
    cw = cw_ref[...]
    xc = cbias_ref[...] + cw[CONV_W - 1:CONV_W, :] * xs_scr[8:, :]
    for j in range(CONV_W - 1):
        xc = xc + cw[j:j + 1, :] * xs_scr[pl.ds(8 - (CONV_W - 1) + j, tt), :]
    xcb = xc.astype(BF)
    rg, ig = [], []
    for gi in range(LRU_W // LRU_GROUP):
        gs = slice(gi * LRU_GROUP, (gi + 1) * LRU_GROUP)
        rg.append(_dot(xcb[:, gs], wr_ref[gi]))
        ig.append(_dot(xcb[:, gs], wi_ref[gi]))
    rgate = jax.nn.sigmoid(jnp.concatenate(rg, axis=1) + br_ref[...])
    igate = jax.nn.sigmoid(jnp.concatenate(ig, axis=1) + bi_ref[...])
    log_a = LRU_C * rgate * _log_sigmoid(lam_ref[...])
    a = jnp.exp(log_a)
    u = jnp.sqrt(-jnp.tanh(log_a) * (a * a + 1.0)) * (igate * xc)

    sub = lax.broadcasted_iota(jnp.int32, (8, LRU_W), 0)
    carry = h_scr[...]
    groups = []
    for gi in range(tt // 8):
        ag, ug = a[8 * gi:8 * gi + 8, :], u[8 * gi:8 * gi + 8, :]
        for step in (1, 2, 4):
            keep = sub >= step
            a_sh = jnp.where(keep, pltpu.roll(ag, step, 0), 1.0)
            u_sh = jnp.where(keep, pltpu.roll(ug, step, 0), 0.0)
            ug = ag * u_sh + ug
            ag = ag * a_sh
        hg = ag * carry + ug
        carry = hg[7:8, :]
        groups.append(hg)
    hs = jnp.concatenate(groups, axis=0)
    h_scr[...] = carry
    y_ref[...] = (hs * _gelu_tanh(gate_ref[...].astype(F32))).astype(BF)
    xs_scr[0:8, :] = xs_scr[tt:tt + 8, :]

    @pl.when(i == pl.num_programs(1) - 1)
    def _():
        hfin_ref[0] = carry
        cfin_ref[0] = xs_scr[tt + 8 - (CONV_W - 1):tt + 8, :]


def _lru_mix(xr, gate, h0, cbuf, cw, cbias, wr, br, wi, bi, lam, batch, seq):
    tt = min(seq, 256)
    nt = seq // tt
    assert seq % tt == 0 and tt % 8 == 0
    row = lambda n: pl.BlockSpec((tt, n), lambda b, i: (b * nt + i, 0))
    per_b = lambda n: pl.BlockSpec((1, n, LRU_W), lambda b, i: (b, 0, 0))
    const = lambda t: pl.BlockSpec(t.shape, lambda b, i: (0,) * t.ndim)
    return pl.pallas_call(
        functools.partial(_lru_kernel, tt=tt),
        grid=(batch, nt),
        in_specs=[row(LRU_W), row(LRU_W), per_b(1), per_b(CONV_W - 1), const(cw), const(cbias),
                  const(wr), const(br), const(wi), const(bi), const(lam)],
        out_specs=[row(LRU_W), per_b(1), per_b(CONV_W - 1)],
        out_shape=[jax.ShapeDtypeStruct((batch * seq, LRU_W), BF),
                   jax.ShapeDtypeStruct((batch, 1, LRU_W), F32),
                   jax.ShapeDtypeStruct((batch, CONV_W - 1, LRU_W), F32)],
        scratch_shapes=[pltpu.VMEM((tt + 8, LRU_W), F32), pltpu.VMEM((1, LRU_W), F32)],
        compiler_params=_cparams("parallel", "arbitrary"),
        name="lru_mix",
    )(xr, gate, h0, cbuf, cw, cbias, wr, br, wi, bi, lam)


def _group_block_diag(w):
    per = LRU_GROUP // LRU_BLOCK
    wg = w.reshape(LRU_W // LRU_GROUP, per, LRU_BLOCK, LRU_BLOCK)
    eye = jnp.eye(per, dtype=w.dtype)
    out = jnp.einsum('gpij,pq->gpiqj', wg, eye)
    return out.reshape(LRU_W // LRU_GROUP, LRU_GROUP, LRU_GROUP).astype(BF)


def _row(v):
    return v.reshape(1, -1).astype(F32)


def kernel(x_prompt, x_sample, state_l0_gla, cache_l1_k, cache_l1_v, state_l2_h, state_l2_conv, state_l3_gla, l0_norm_mix, l0_gla_w_in, l0_gla_w_gate_a, l0_gla_w_gate_b, l0_gla_b_gate, l0_gla_head_norm, l0_gla_w_out, l0_norm_ffn, l0_ffn_w_in, l0_ffn_w_out, l1_norm_mix, l1_att_w_qkv, l1_att_rel_bias, l1_att_w_out, l1_norm_ffn, l1_ffn_w_in, l1_ffn_w_out, l2_norm_mix, l2_lru_w_in, l2_lru_conv_w, l2_lru_conv_b, l2_lru_w_rgate, l2_lru_b_rgate, l2_lru_w_igate, l2_lru_b_igate, l2_lru_lambda, l2_lru_w_out, l2_norm_ffn, l2_ffn_w_in, l2_ffn_w_out, l3_norm_mix, l3_gla_w_in, l3_gla_w_gate_a, l3_gla_w_gate_b, l3_gla_b_gate, l3_gla_head_norm, l3_gla_w_out, l3_norm_ffn, l3_ffn_w_in, l3_ffn_w_out, final_norm):
    bp, tp, _ = x_prompt.shape
    bs, ts, _ = x_sample.shape
    streams = [(x_prompt.reshape(bp * tp, D_MODEL), bp, tp), (x_sample.reshape(bs * ts, D_MODEL), bs, ts)]
    tm_proj = TM_PROJ
    tm_post = TM_POST

    def post(xs, mixed, wo, gn, w1, w2, fn=None):
        wo_b, w1_b, w2_b = wo.astype(BF), w1.astype(BF), w2.astype(BF)
        fn_r = None if fn is None else _row(fn)
        return [_post(a, x, wo_b, _row(gn), w1_b, w2_b, fn_r, _row_tile(x.shape[0], tm_post))
                for a, x in zip(mixed, xs)]

    def gla_layer(xs, states, norm_mix, w_in, wa, wb, bg, hn, w_out, norm_ffn, f_in, f_out, fn=None):
        wa_p = jnp.pad(wa, ((0, 0), (0, LANE - GLA_RANK))).astype(BF)
        wb_p = jnp.pad(wb, ((0, LANE - GLA_RANK), (0, 0))).astype(BF)
        w_in_b = w_in.astype(BF)
        mixed, new_states = [], []
        for (x, b, t), s0 in zip(xs, states):
            q, k, v, r, g = _gla_proj(x, _row(norm_mix), w_in_b, wa_p, wb_p, _row(bg), _row_tile(x.shape[0], tm_proj))
            o, s_new = _gla_mix(q, k, v, r, g, s0.astype(F32), _row(hn), b, t)
            mixed.append(o)
            new_states.append(s_new)
        outs = post([x for x, _, _ in xs], mixed, w_out, norm_ffn, f_in, f_out, fn)
        return [(o, b, t) for o, (_, b, t) in zip(outs, xs)], new_states

    zeros_gla = jnp.zeros((bp, GLA_HEADS, GLA_DK, GLA_DV), F32)
    streams, (gla0_p, gla0_s) = gla_layer(
        streams, [zeros_gla, state_l0_gla], l0_norm_mix, l0_gla_w_in, l0_gla_w_gate_a, l0_gla_w_gate_b,
        l0_gla_b_gate, l0_gla_head_norm, l0_gla_w_out, l0_norm_ffn, l0_ffn_w_in, l0_ffn_w_out)

    n_cache = cache_l1_k.shape[1]
    n_keys = n_cache + ts
    assert n_cache == ATT_BAND and n_keys <= ATT_KW and ts <= ATT_QT
    bias_p = _rel_bias(l1_att_rel_bias, ATT_QT,
                       lambda t, r: ((t < CHUNK) & (r < ATT_BAND + CHUNK)) | ((t >= CHUNK) & (r >= CHUNK)))
    bias_s = _rel_bias(l1_att_rel_bias, ts, lambda t, r: (r < n_keys) & (t >= 0))
    w_qkv = l1_att_w_qkv.astype(BF)
    (xp, _, _), (xs_, _, _) = streams
    qkv_scales = (ATT_DH ** -0.5, 1.0, 1.0)
    qp, kp, vp = _split_proj(xp, _row(l1_norm_mix), w_qkv, [D_MODEL] * 3, [BF] * 3, _row_tile(bp * tp, tm_proj),
                             "qkv_proj", qkv_scales)
    qs, ks, vs = _split_proj(xs_, _row(l1_norm_mix), w_qkv, [D_MODEL] * 3, [BF] * 3, _row_tile(bs * ts, tm_proj),
                             "qkv_proj", qkv_scales)
    op = _attn_prompt(qp, kp, vp, bias_p, bp, tp)

    def window(cache, new):
        c = cache.reshape(bs, n_cache, D_MODEL).astype(BF)
        z = jnp.zeros((bs, ATT_KW - n_keys, D_MODEL), BF)
        return jnp.concatenate([c, new.reshape(bs, ts, D_MODEL), z], axis=1)

    os_ = _attn_sample(qs.reshape(bs, ts, D_MODEL), window(cache_l1_k, ks), window(cache_l1_v, vs), bias_s)
    keep = min(ATT_BAND, tp)
    k1_p = kp.reshape(bp, tp, ATT_HEADS, ATT_DH)[:, tp - keep:].astype(F32)
    v1_p = vp.reshape(bp, tp, ATT_HEADS, ATT_DH)[:, tp - keep:].astype(F32)
    k1_s = ks.reshape(bs, ts, ATT_HEADS, ATT_DH).astype(F32)
    v1_s = vs.reshape(bs, ts, ATT_HEADS, ATT_DH).astype(F32)
    outs = post([xp, xs_], [op, os_.reshape(bs * ts, D_MODEL)], l1_att_w_out, l1_norm_ffn, l1_ffn_w_in, l1_ffn_w_out)
    streams = [(outs[0], bp, tp), (outs[1], bs, ts)]

    w_lru = l2_lru_w_in.astype(BF)
    wr = _group_block_diag(l2_lru_w_rgate)
    wi = _group_block_diag(l2_lru_w_igate)
    lru_states = [(jnp.zeros((bp, 1, LRU_W), F32), jnp.zeros((bp, CONV_W - 1, LRU_W), F32)),
                  (state_l2_h.reshape(bs, 1, LRU_W).astype(F32), state_l2_conv.astype(F32))]
    mixed, lru_new = [], []
    for (x, b, t), (h0, cbuf) in zip(streams, lru_states):
        gate, xr = _split_proj(x, _row(l2_norm_mix), w_lru, [LRU_W] * 2, [BF, F32], _row_tile(x.shape[0], tm_proj), "lru_proj")
        y, hfin, cfin = _lru_mix(xr, gate, h0, cbuf, l2_lru_conv_w.astype(F32), _row(l2_lru_conv_b), wr,
                                 _row(l2_lru_b_rgate), wi, _row(l2_lru_b_igate), _row(l2_lru_lambda), b, t)
        mixed.append(y)
        lru_new.append((hfin.reshape(b, LRU_W), cfin))
    outs = post([x for x, _, _ in streams], mixed, l2_lru_w_out, l2_norm_ffn, l2_ffn_w_in, l2_ffn_w_out)
    streams = [(outs[0], bp, tp), (outs[1], bs, ts)]
    (h2_p, conv2_p), (h2_s, conv2_s) = lru_new

    streams, (gla3_p, gla3_s) = gla_layer(
        streams, [zeros_gla, state_l3_gla], l3_norm_mix, l3_gla_w_in, l3_gla_w_gate_a, l3_gla_w_gate_b,
        l3_gla_b_gate, l3_gla_head_norm, l3_gla_w_out, l3_norm_ffn, l3_ffn_w_in, l3_ffn_w_out, fn=final_norm)

    y_prompt = streams[0][0].reshape(bp, tp, D_MODEL)
    y_sample = streams[1][0].reshape(bs, ts, D_MODEL)
    return (y_prompt, y_sample, gla0_p, gla0_s, k1_p, v1_p, k1_s, v1_s,
            h2_p, conv2_p, h2_s, conv2_s, gla3_p, gla3_s)
```

```python
import functools
import math

import numpy as np
import jax
import jax.numpy as jnp
from jax import lax
from jax.experimental import pallas as pl
from jax.experimental.pallas import tpu as pltpu

F32 = jnp.float32
BF = jnp.bfloat16

D_MODEL = 1024
CHUNK = 64
EPS = 1e-6
NEG_INF = -1e30
LOG2E = 1.4426950408889634

GLA_HEADS = 4
GLA_DK = 128
GLA_DV = 256
GLA_RANK = 16
GLA_TAU = 16.0
GLA_HK = GLA_HEADS * GLA_DK
GLA_HV = GLA_HEADS * GLA_DV
GLA_TILE_CHUNKS = 4
GLA_ROWS_PER_STEP = 2

ATT_HEADS = 16
ATT_DH = 64
ATT_BAND = 512
ATT_MAX_REL = 128
ATT_QT = 128
ATT_KW = ATT_BAND + ATT_QT
ATT_BLK = 512
ATT_TBL = 384
ATT_LOOKAHEAD = 2

LRU_W = 1536
LRU_BLOCK = 96
LRU_GROUP = 384
LRU_C = 8.0
CONV_W = 4

D_FF = 2816

LANE = 128
TM_PROJ = 512
TM_POST = 256
VMEM_LIMIT_BYTES = 56 * 1024 * 1024


def _cparams(*sem):
    return pltpu.CompilerParams(dimension_semantics=sem, vmem_limit_bytes=VMEM_LIMIT_BYTES)


def _resident(shape):
    nd = len(shape)
    return pl.BlockSpec(shape, lambda *_: (0,) * nd, pipeline_mode=pl.Buffered(1))


def _dot(a, b):
    return jnp.dot(a, b, preferred_element_type=F32)


def _dot_nt(a, b):
    return lax.dot_general(a, b, (((1,), (1,)), ((), ())), preferred_element_type=F32)


def _dot_tn(a, b):
    return lax.dot_general(a, b, (((0,), (0,)), ((), ())), preferred_element_type=F32)


def _rms(x, g):
    return x * lax.rsqrt(jnp.mean(x * x, axis=-1, keepdims=True) + EPS) * g


def _log_sigmoid(z):
    return jnp.minimum(z, 0.0) - jnp.log1p(jnp.exp(-jnp.abs(z)))


def _silu(x):
    return x * jax.nn.sigmoid(x)


def _gelu_tanh(x):
    return x * (0.5 * (1.0 + jnp.tanh(math.sqrt(2.0 / math.pi) * (x + 0.044715 * (x * x * x)))))


def _row_tile(m, pref):
    t = min(m, pref)
    assert m % t == 0
    return t


def _gla_proj_kernel(x_ref, gn_ref, w_ref, wa_ref, wb_ref, bg_ref, q_ref, k_ref, v_ref, r_ref, g_ref):
    h = _rms(x_ref[...], gn_ref[...]).astype(BF)
    low = _dot(h, wa_ref[...]).astype(BF)
    z = _dot(low, wb_ref[...]) + bg_ref[...]
    q_ref[...] = _dot(h, w_ref[:, 0:GLA_HK]).astype(BF)
    k_ref[...] = _dot(h, w_ref[:, GLA_HK:2 * GLA_HK]).astype(BF)
    g_ref[...] = _log_sigmoid(z) * (1.0 / GLA_TAU)
    v_ref[...] = _dot(h, w_ref[:, 2 * GLA_HK:2 * GLA_HK + GLA_HV]).astype(BF)
    r_ref[...] = _dot(h, w_ref[:, 2 * GLA_HK + GLA_HV:]).astype(BF)


def _gla_proj(x2, gn, w_in, wa, wb, bg, tm):
    m = x2.shape[0]
    row = lambda n: pl.BlockSpec((tm, n), lambda i: (i, 0))
    return pl.pallas_call(
        _gla_proj_kernel,
        grid=(m // tm,),
        in_specs=[row(D_MODEL), _resident(gn.shape), _resident(w_in.shape), _resident(wa.shape),
                  _resident(wb.shape), _resident(bg.shape)],
        out_specs=[row(GLA_HK), row(GLA_HK), row(GLA_HV), row(GLA_HV), row(GLA_HK)],
        out_shape=[jax.ShapeDtypeStruct((m, GLA_HK), BF), jax.ShapeDtypeStruct((m, GLA_HK), BF),
                   jax.ShapeDtypeStruct((m, GLA_HV), BF), jax.ShapeDtypeStruct((m, GLA_HV), BF),
                   jax.ShapeDtypeStruct((m, GLA_HK), F32)],
        compiler_params=_cparams("parallel"),
        name="gla_proj",
    )(x2, gn, w_in, wa, wb, bg)


def _split_proj_kernel(x_ref, gn_ref, w_ref, *out_refs, scales):
    h = _rms(x_ref[...], gn_ref[...]).astype(BF)
    col = 0
    for o_ref, scale in zip(out_refs, scales):
        n = o_ref.shape[1]
        acc = _dot(h, w_ref[:, col:col + n])
        o_ref[...] = (acc if scale == 1.0 else acc * scale).astype(o_ref.dtype)
        col += n


def _split_proj(x2, gn, w, widths, dtypes, tm, name, scales=None):
    scales = tuple(scales) if scales is not None else (1.0,) * len(widths)
    m = x2.shape[0]
    row = lambda n: pl.BlockSpec((tm, n), lambda i: (i, 0))
    return pl.pallas_call(
        functools.partial(_split_proj_kernel, scales=scales),
        grid=(m // tm,),
        in_specs=[row(D_MODEL), _resident(gn.shape), _resident(w.shape)],
        out_specs=[row(n) for n in widths],
        out_shape=[jax.ShapeDtypeStruct((m, n), dt) for n, dt in zip(widths, dtypes)],
        compiler_params=_cparams("parallel"),
        name=name,
    )(x2, gn, w)


def _post_kernel(*refs, final):
    if final:
        a_ref, x_ref, wo_ref, gn_ref, w1_ref, w2_ref, fn_ref, out_ref = refs
    else:
        a_ref, x_ref, wo_ref, gn_ref, w1_ref, w2_ref, out_ref = refs
    x1 = x_ref[...] + _dot(a_ref[...], wo_ref[...])
    h = _rms(x1, gn_ref[...]).astype(BF)
    g = _dot(h, w1_ref[:, 0:D_FF])
    u = _dot(h, w1_ref[:, D_FF:])
    act = (_silu(g) * u).astype(BF)
    x2 = x1 + _dot(act, w2_ref[...])
    if final:
        x2 = _rms(x2, fn_ref[...])
    out_ref[...] = x2


def _post(a2, x2, wo, gn, w1, w2, fn, tm):
    m = x2.shape[0]
    ka = a2.shape[1]
    final = fn is not None
    row = lambda n: pl.BlockSpec((tm, n), lambda i: (i, 0))
    ins = [a2, x2, wo, gn, w1, w2] + ([fn] if final else [])
    specs = [row(ka), row(D_MODEL)] + [_resident(t.shape) for t in ins[2:]]
    return pl.pallas_call(
        functools.partial(_post_kernel, final=final),
        grid=(m // tm,),
        in_specs=specs,
        out_specs=row(D_MODEL),
        out_shape=jax.ShapeDtypeStruct((m, D_MODEL), F32),
        compiler_params=_cparams("parallel"),
        name="post_final" if final else "post",
    )(*ins)


def _gla_stream(sb, q_ref, k_ref, v_ref, r_ref, g_ref, hn, s_scr, o_ref, tril, a_mask, chunk):
    tt = q_ref.shape[1]
    n_chunks = tt // chunk
    heads = [(slice(hd * GLA_DK, (hd + 1) * GLA_DK), slice(hd * GLA_DV, (hd + 1) * GLA_DV))
             for hd in range(GLA_HEADS)]
    rows = [slice(c * chunk, (c + 1) * chunk) for c in range(n_chunks)]

    g = g_ref[sb]
    g_hi = g.astype(BF)
    g_lo = (g - g_hi.astype(F32)).astype(BF)
    b = _dot(tril, g_hi) + _dot(tril, g_lo)
    yield

    b_last = [b[rs.stop - 1:rs.stop, :] for rs in rows]
    b_end = jnp.concatenate([jnp.broadcast_to(bl, (chunk, GLA_HK)) for bl in b_last], axis=0)
    k = k_ref[sb].astype(F32)
    q_e = ((q_ref[sb].astype(F32) * (GLA_DK ** -0.5)) * jnp.exp(b)).astype(BF)
    k_e = (k * jnp.exp(-b)).astype(BF)
    k_s = (k * jnp.exp(b_end - b)).astype(BF)
    scores = [_dot_nt(q_e[:, ks], k_e[:, ks]) for ks, _ in heads]
    kv = [[_dot_tn(k_s[rs, ks], v_ref[sb, rs, vs]) for ks, vs in heads] for rs in rows]
    yield

    o_intra = [_dot(jnp.where(a_mask, sc, 0.0).astype(BF), v_ref[sb, :, vs]) for sc, (_, vs) in zip(scores, heads)]
    state = [s_scr[sb, hd] for hd in range(GLA_HEADS)]
    o_inter = []
    for c, rs in enumerate(rows):
        o_inter.append([_dot(q_e[rs, ks], state[hd].astype(BF)) for hd, (ks, _) in enumerate(heads)])
        decay = jnp.transpose(jnp.broadcast_to(jnp.exp(b_last[c]), (LANE, GLA_HK)))
        for hd, (ks, _) in enumerate(heads):
            d = decay[ks, :]
            state[hd] = jnp.concatenate([d, d], axis=1) * state[hd] + kv[c][hd]
    for hd in range(GLA_HEADS):
        s_scr[sb, hd] = state[hd]
    yield

    for hd, (_, vs) in enumerate(heads):
        o = o_intra[hd] + jnp.concatenate([o_inter[c][hd] for c in range(n_chunks)], axis=0)
        o = o * lax.rsqrt(jnp.mean(o * o, axis=-1, keepdims=True) + EPS) * hn
        o_ref[sb, :, vs] = (o * _silu(r_ref[sb, :, vs].astype(F32))).astype(BF)
    yield


def _gla_kernel(q_ref, k_ref, v_ref, r_ref, g_ref, s0_ref, hn_ref, o_ref, sfin_ref, s_scr, *, chunk):
    i = pl.program_id(1)
    nb, tt = q_ref.shape[0], q_ref.shape[1]

    @pl.when(i == 0)
    def _():
        s_scr[...] = s0_ref[...]

    t_row = lax.broadcasted_iota(jnp.int32, (tt, tt), 0)
    t_col = lax.broadcasted_iota(jnp.int32, (tt, tt), 1)
    a_mask = t_row >= t_col
    for c in range(1, tt // chunk):
        a_mask = a_mask & ((t_row < c * chunk) | (t_col >= c * chunk))
    tril = jnp.where(a_mask, 1.0, 0.0).astype(BF)
    hn = hn_ref[...]

    streams = [_gla_stream(sb, q_ref, k_ref, v_ref, r_ref, g_ref, hn, s_scr, o_ref, tril, a_mask, chunk)
               for sb in range(nb)]
    for _ in range(4):
        for st in streams:
            next(st)

    @pl.when(i == pl.num_programs(1) - 1)
    def _():
        sfin_ref[...] = s_scr[...]


def _gla_mix(q, k, v, r, g, s0, hn, batch, seq):
    chunk = min(seq, CHUNK)
    tt = min(seq, GLA_TILE_CHUNKS * chunk)
    nb = GLA_ROWS_PER_STEP if batch % GLA_ROWS_PER_STEP == 0 else 1
    assert seq % tt == 0 and tt % chunk == 0
    blk = lambda n: pl.BlockSpec((nb, tt, n), lambda b, i: (b, i, 0))
    st = pl.BlockSpec((nb, GLA_HEADS, GLA_DK, GLA_DV), lambda b, i: (b, 0, 0, 0))
    as3 = lambda t: t.reshape(batch, seq, t.shape[-1])
    o, s_new = pl.pallas_call(
        functools.partial(_gla_kernel, chunk=chunk),
        grid=(batch // nb, seq // tt),
        in_specs=[blk(GLA_HK), blk(GLA_HK), blk(GLA_HV), blk(GLA_HV), blk(GLA_HK), st,
                  pl.BlockSpec(hn.shape, lambda b, i: (0, 0))],
        out_specs=[blk(GLA_HV), st],
        out_shape=[jax.ShapeDtypeStruct((batch, seq, GLA_HV), BF),
                   jax.ShapeDtypeStruct((batch, GLA_HEADS, GLA_DK, GLA_DV), F32)],
        scratch_shapes=[pltpu.VMEM((nb, GLA_HEADS, GLA_DK, GLA_DV), F32)],
        compiler_params=_cparams("parallel", "arbitrary"),
        name="gla_mix",
    )(as3(q), as3(k), as3(v), as3(r), as3(g), s0, hn)
    return o.reshape(batch * seq, GLA_HV), s_new


def _bias_kernel(idx_ref, madd_ref, thi_ref, tmid_ref, tlo_ref, out_ref):
    d = lax.broadcasted_iota(jnp.int32, (ATT_TBL, idx_ref.shape[1]), 0)
    onehot = jnp.where(d == idx_ref[...], 1.0, 0.0).astype(BF)
    b = _dot(thi_ref[...], onehot) + _dot(tmid_ref[...], onehot) + _dot(tlo_ref[...], onehot)
    out_ref[...] = b * LOG2E + madd_ref[...]


def _rel_bias(table, nq, visible):
    t = np.arange(nq)[:, None]
    r = np.arange(ATT_KW)[None, :]
    idx = np.clip(ATT_BAND + t - r, -ATT_MAX_REL, ATT_MAX_REL) + ATT_MAX_REL
    idx = jnp.asarray(idx.reshape(1, nq * ATT_KW), jnp.int32)
    madd = jnp.asarray(np.where(visible(t, r), 0.0, NEG_INF).reshape(1, nq * ATT_KW), F32)
    tpad = jnp.pad(table.astype(F32), ((0, 0), (0, ATT_TBL - table.shape[1])))
    t_hi = tpad.astype(BF)
    t_mid = (tpad - t_hi.astype(F32)).astype(BF)
    t_lo = (tpad - t_hi.astype(F32) - t_mid.astype(F32)).astype(BF)
    cols = 2 * ATT_KW
    tab = pl.BlockSpec((ATT_HEADS, ATT_TBL), lambda i: (0, 0))
    vec = pl.BlockSpec((1, cols), lambda i: (0, i))
    out = pl.pallas_call(
        _bias_kernel,
        grid=(nq * ATT_KW // cols,),
        in_specs=[vec, vec, tab, tab, tab],
        out_specs=pl.BlockSpec((ATT_HEADS, cols), lambda i: (0, i)),
        out_shape=jax.ShapeDtypeStruct((ATT_HEADS, nq * ATT_KW), F32),
        compiler_params=_cparams("parallel"),
        name="rel_bias",
    )(idx, madd, t_hi, t_mid, t_lo)
    return out.reshape(ATT_HEADS, nq, ATT_KW)


def _att_scores(q2, kw, sub):
    lane = lax.broadcasted_iota(jnp.int32, q2.shape, 1)
    mine = (lane < ATT_DH) if sub == 0 else (lane >= ATT_DH)
    return _dot_nt(jnp.where(mine, q2, jnp.zeros_like(q2)), kw)


def _att_probs(s, bias2, visible):
    s2 = s * LOG2E + bias2
    if visible is not None:
        s2 = jnp.where(visible, s2, NEG_INF)
    p = jnp.exp2(s2 - jnp.max(s2, axis=-1, keepdims=True))
    return p.astype(BF), 1.0 / jnp.sum(p, axis=-1, keepdims=True)


def _att_pipeline(items, scores, probs, finish):
    n = len(items)
    queue = [scores(items[j]) for j in range(min(ATT_LOOKAHEAD, n))]
    pending = None
    for j, item in enumerate(items):
        if j + ATT_LOOKAHEAD < n:
            queue.append(scores(items[j + ATT_LOOKAHEAD]))
        p_inv = probs(item, queue.pop(0))
        if pending is not None:
            finish(*pending)
        pending = (item, p_inv)
    finish(*pending)


def _attn_prompt_kernel(q_ref, kp_ref, kc_ref, vp_ref, vc_ref, bias_ref, o_ref):
    items = [(jj, hp, sub) for jj in range(ATT_BLK // ATT_QT) for hp in range(ATT_HEADS // 2) for sub in range(2)]
    r = lax.broadcasted_iota(jnp.int32, (ATT_QT, ATT_KW), 1)
    lane = lax.broadcasted_iota(jnp.int32, (ATT_QT, LANE), 1)

    def window(prev_ref, cur_ref, jj, hp):
        lo, ls = jj * ATT_QT, slice(hp * LANE, (hp + 1) * LANE)
        return jnp.concatenate([prev_ref[lo:, ls], cur_ref[0:lo + ATT_QT, ls]], axis=0)

    def scores(item):
        jj, hp, sub = item
        return _att_scores(q_ref[jj * ATT_QT:(jj + 1) * ATT_QT, hp * LANE:(hp + 1) * LANE],
                           window(kp_ref, kc_ref, jj, hp), sub)

    def body(first_block):
        held = {}

        def probs(item, s):
            jj, hp, sub = item
            visible = (r >= ATT_BLK - jj * ATT_QT) if first_block else None
            return _att_probs(s, bias_ref[2 * hp + sub], visible)

        def finish(item, p_inv):
            jj, hp, sub = item
            o = _dot(p_inv[0], window(vp_ref, vc_ref, jj, hp)) * p_inv[1]
            if sub == 0:
                held[0] = o
            else:
                o_ref[jj * ATT_QT:(jj + 1) * ATT_QT, hp * LANE:(hp + 1) * LANE] = (
                    jnp.where(lane < ATT_DH, held[0], o).astype(BF))

        _att_pipeline(items, scores, probs, finish)

    i = pl.program_id(1)
    pl.when(i == 0)(lambda: body(True))
    pl.when(i > 0)(lambda: body(False))


def _attn_prompt(q, k, v, bias, batch, seq):
    nt = seq // ATT_BLK
    assert seq % ATT_BLK == 0
    cur = pl.BlockSpec((ATT_BLK, D_MODEL), lambda b, i: (b * nt + i, 0))
    prev = pl.BlockSpec((ATT_BLK, D_MODEL), lambda b, i: (b * nt + jnp.maximum(i - 1, 0), 0))
    return pl.pallas_call(
        _attn_prompt_kernel,
        grid=(batch, nt),
        in_specs=[cur, prev, cur, prev, cur, _resident(bias.shape)],
        out_specs=cur,
        out_shape=jax.ShapeDtypeStruct((batch * seq, D_MODEL), BF),
        compiler_params=_cparams("parallel", "arbitrary"),
        name="attn_prompt",
    )(q, k, k, v, v, bias)


def _attn_sample_kernel(q_ref, kw_ref, vw_ref, bias_ref, o_ref):
    nq = q_ref.shape[1]
    items = [(hp, sub) for hp in range(ATT_HEADS // 2) for sub in range(2)]
    lane = lax.broadcasted_iota(jnp.int32, (nq, LANE), 1)
    held = {}

    def scores(item):
        hp, sub = item
        ls = slice(hp * LANE, (hp + 1) * LANE)
        return _att_scores(q_ref[0, :, ls], kw_ref[0, :, ls], sub)

    def probs(item, s):
        hp, sub = item
        return _att_probs(s, bias_ref[2 * hp + sub], None)

    def finish(item, p_inv):
        hp, sub = item
        ls = slice(hp * LANE, (hp + 1) * LANE)
        o = _dot(p_inv[0], vw_ref[0, :, ls]) * p_inv[1]
        if sub == 0:
            held[0] = o
        else:
            o_ref[0, :, ls] = jnp.where(lane < ATT_DH, held[0], o).astype(BF)

    _att_pipeline(items, scores, probs, finish)


def _attn_sample(q3, kw3, vw3, bias):
    batch, nq, _ = q3.shape
    blk = lambda n: pl.BlockSpec((1, n, D_MODEL), lambda b: (b, 0, 0))
    return pl.pallas_call(
        _attn_sample_kernel,
        grid=(batch,),
        in_specs=[blk(nq), blk(ATT_KW), blk(ATT_KW), _resident(bias.shape)],
        out_specs=blk(nq),
        out_shape=jax.ShapeDtypeStruct((batch, nq, D_MODEL), BF),
        compiler_params=_cparams("parallel"),
        name="attn_sample",
    )(q3, kw3, vw3, bias)


def _lru_kernel(xr_ref, gate_ref, h0_ref, cb_ref, cw_ref, cbias_ref, wr_ref, br_ref, wi_ref, bi_ref, lam_ref,
                y_ref, hfin_ref, cfin_ref, xs_scr, h_scr, *, tt):
    i = pl.program_id(1)

    @pl.when(i == 0)
    def _():
        xs_scr[0:8, :] = jnp.zeros((8, LRU_W), F32)
        xs_scr[8 - (CONV_W - 1):8, :] = cb_ref[0]
        h_scr[...] = h0_ref[0]

    xs_scr[8:, :] = xr_ref[...]
    cw = cw_ref[...]
    xc = cbias_ref[...] + cw[CONV_W - 1:CONV_W, :] * xs_scr[8:, :]
    for j in range(CONV_W - 1):
        xc = xc + cw[j:j + 1, :] * xs_scr[pl.ds(8 - (CONV_W - 1) + j, tt), :]
    xcb = xc.astype(BF)
    rg, ig = [], []
    for gi in range(LRU_W // LRU_GROUP):
        gs = slice(gi * LRU_GROUP, (gi + 1) * LRU_GROUP)
        rg.append(_dot(xcb[:, gs], wr_ref[gi]))
        ig.append(_dot(xcb[:, gs], wi_ref[gi]))
    rgate = jax.nn.sigmoid(jnp.concatenate(rg, axis=1) + br_ref[...])
    igate = jax.nn.sigmoid(jnp.concatenate(ig, axis=1) + bi_ref[...])
    log_a = LRU_C * rgate * _log_sigmoid(lam_ref[...])
    a = jnp.exp(log_a)
    u = jnp.sqrt(-jnp.tanh(log_a) * (a * a + 1.0)) * (igate * xc)

    sub = lax.broadcasted_iota(jnp.int32, (8, LRU_W), 0)
    carry = h_scr[...]
    groups = []
    for gi in range(tt // 8):
        ag, ug = a[8 * gi:8 * gi + 8, :], u[8 * gi:8 * gi + 8, :]
        for step in (1, 2, 4):
            keep = sub >= step
            a_sh = jnp.where(keep, pltpu.roll(ag, step, 0), 1.0)
            u_sh = jnp.where(keep, pltpu.roll(ug, step, 0), 0.0)
            ug = ag * u_sh + ug
            ag = ag * a_sh
        hg = ag * carry + ug
        carry = hg[7:8, :]
        groups.append(hg)
    hs = jnp.concatenate(groups, axis=0)
    h_scr[...] = carry
    y_ref[...] = (hs * _gelu_tanh(gate_ref[...].astype(F32))).astype(BF)
    xs_scr[0:8, :] = xs_scr[tt:tt + 8, :]

    @pl.when(i == pl.num_programs(1) - 1)
    def _():
        hfin_ref[0] = carry
        cfin_ref[0] = xs_scr[tt + 8 - (CONV_W - 1):tt + 8, :]


def _lru_mix(xr, gate, h0, cbuf, cw, cbias, wr, br, wi, bi, lam, batch, seq):
    tt = min(seq, 256)
    nt = seq // tt
    assert seq % tt == 0 and tt % 8 == 0
    row = lambda n: pl.BlockSpec((tt, n), lambda b, i: (b * nt + i, 0))
    per_b = lambda n: pl.BlockSpec((1, n, LRU_W), lambda b, i: (b, 0, 0))
    const = lambda t: pl.BlockSpec(t.shape, lambda b, i: (0,) * t.ndim)
    return pl.pallas_call(
        functools.partial(_lru_kernel, tt=tt),
        grid=(batch, nt),
        in_specs=[row(LRU_W), row(LRU_W), per_b(1), per_b(CONV_W - 1), const(cw), const(cbias),
                  const(wr), const(br), const(wi), const(bi), const(lam)],
        out_specs=[row(LRU_W), per_b(1), per_b(CONV_W - 1)],
        out_shape=[jax.ShapeDtypeStruct((batch * seq, LRU_W), BF),
                   jax.ShapeDtypeStruct((batch, 1, LRU_W), F32),
                   jax.ShapeDtypeStruct((batch, CONV_W - 1, LRU_W), F32)],
        scratch_shapes=[pltpu.VMEM((tt + 8, LRU_W), F32), pltpu.VMEM((1, LRU_W), F32)],
        compiler_params=_cparams("parallel", "arbitrary"),
        name="lru_mix",
    )(xr, gate, h0, cbuf, cw, cbias, wr, br, wi, bi, lam)


def _group_block_diag(w):
    per = LRU_GROUP // LRU_BLOCK
    wg = w.reshape(LRU_W // LRU_GROUP, per, LRU_BLOCK, LRU_BLOCK)
    eye = jnp.eye(per, dtype=w.dtype)
    out = jnp.einsum('gpij,pq->gpiqj', wg, eye)
    return out.reshape(LRU_W // LRU_GROUP, LRU_GROUP, LRU_GROUP).astype(BF)


def _row(v):
    return v.reshape(1, -1).astype(F32)


def kernel(x_prompt, x_sample, state_l0_gla, cache_l1_k, cache_l1_v, state_l2_h, state_l2_conv, state_l3_gla, l0_norm_mix, l0_gla_w_in, l0_gla_w_gate_a, l0_gla_w_gate_b, l0_gla_b_gate, l0_gla_head_norm, l0_gla_w_out, l0_norm_ffn, l0_ffn_w_in, l0_ffn_w_out, l1_norm_mix, l1_att_w_qkv, l1_att_rel_bias, l1_att_w_out, l1_norm_ffn, l1_ffn_w_in, l1_ffn_w_out, l2_norm_mix, l2_lru_w_in, l2_lru_conv_w, l2_lru_conv_b, l2_lru_w_rgate, l2_lru_b_rgate, l2_lru_w_igate, l2_lru_b_igate, l2_lru_lambda, l2_lru_w_out, l2_norm_ffn, l2_ffn_w_in, l2_ffn_w_out, l3_norm_mix, l3_gla_w_in, l3_gla_w_gate_a, l3_gla_w_gate_b, l3_gla_b_gate, l3_gla_head_norm, l3_gla_w_out, l3_norm_ffn, l3_ffn_w_in, l3_ffn_w_out, final_norm):
    bp, tp, _ = x_prompt.shape
    bs, ts, _ = x_sample.shape
    streams = [(x_prompt.reshape(bp * tp, D_MODEL), bp, tp), (x_sample.reshape(bs * ts, D_MODEL), bs, ts)]
    tm_proj = TM_PROJ
    tm_post = TM_POST

    def post(xs, mixed, wo, gn, w1, w2, fn=None):
        wo_b, w1_b, w2_b = wo.astype(BF), w1.astype(BF), w2.astype(BF)
        fn_r = None if fn is None else _row(fn)
        return [_post(a, x, wo_b, _row(gn), w1_b, w2_b, fn_r, _row_tile(x.shape[0], tm_post))
                for a, x in zip(mixed, xs)]

    def gla_layer(xs, states, norm_mix, w_in, wa, wb, bg, hn, w_out, norm_ffn, f_in, f_out, fn=None):
        wa_p = jnp.pad(wa, ((0, 0), (0, LANE - GLA_RANK))).astype(BF)
        wb_p = jnp.pad(wb, ((0, LANE - GLA_RANK), (0, 0))).astype(BF)
        w_in_b = w_in.astype(BF)
        mixed, new_states = [], []
        for (x, b, t), s0 in zip(xs, states):
            q, k, v, r, g = _gla_proj(x, _row(norm_mix), w_in_b, wa_p, wb_p, _row(bg), _row_tile(x.shape[0], tm_proj))
            o, s_new = _gla_mix(q, k, v, r, g, s0.astype(F32), _row(hn), b, t)
            mixed.append(o)
            new_states.append(s_new)
        outs = post([x for x, _, _ in xs], mixed, w_out, norm_ffn, f_in, f_out, fn)
        return [(o, b, t) for o, (_, b, t) in zip(outs, xs)], new_states

    zeros_gla = jnp.zeros((bp, GLA_HEADS, GLA_DK, GLA_DV), F32)
    streams, (gla0_p, gla0_s) = gla_layer(
        streams, [zeros_gla, state_l0_gla], l0_norm_mix, l0_gla_w_in, l0_gla_w_gate_a, l0_gla_w_gate_b,
        l0_gla_b_gate, l0_gla_head_norm, l0_gla_w_out, l0_norm_ffn, l0_ffn_w_in, l0_ffn_w_out)

    n_cache = cache_l1_k.shape[1]
    n_keys = n_cache + ts
    assert n_cache == ATT_BAND and n_keys <= ATT_KW and ts <= ATT_QT
    bias_p = _rel_bias(l1_att_rel_bias, ATT_QT,
                       lambda t, r: ((t < CHUNK) & (r < ATT_BAND + CHUNK)) | ((t >= CHUNK) & (r >= CHUNK)))
    bias_s = _rel_bias(l1_att_rel_bias, ts, lambda t, r: (r < n_keys) & (t >= 0))
    w_qkv = l1_att_w_qkv.astype(BF)
    (xp, _, _), (xs_, _, _) = streams
    qkv_scales = (ATT_DH ** -0.5, 1.0, 1.0)
    qp, kp, vp = _split_proj(xp, _row(l1_norm_mix), w_qkv, [D_MODEL] * 3, [BF] * 3, _row_tile(bp * tp, tm_proj),
                             "qkv_proj", qkv_scales)
    qs, ks, vs = _split_proj(xs_, _row(l1_norm_mix), w_qkv, [D_MODEL] * 3, [BF] * 3, _row_tile(bs * ts, tm_proj),
                             "qkv_proj", qkv_scales)
    op = _attn_prompt(qp, kp, vp, bias_p, bp, tp)

    def window(cache, new):
        c = cache.reshape(bs, n_cache, D_MODEL).astype(BF)
        z = jnp.zeros((bs, ATT_KW - n_keys, D_MODEL), BF)
        return jnp.concatenate([c, new.reshape(bs, ts, D_MODEL), z], axis=1)

    os_ = _attn_sample(qs.reshape(bs, ts, D_MODEL), window(cache_l1_k, ks), window(cache_l1_v, vs), bias_s)
    keep = min(ATT_BAND, tp)
    k1_p = kp.reshape(bp, tp, D_MODEL)[:, tp - keep:].astype(F32).reshape(bp, keep, ATT_HEADS, ATT_DH)
    v1_p = vp.reshape(bp, tp, D_MODEL)[:, tp - keep:].astype(F32).reshape(bp, keep, ATT_HEADS, ATT_DH)
    k1_s = ks.reshape(bs, ts, ATT_HEADS, ATT_DH).astype(F32)
    v1_s = vs.reshape(bs, ts, ATT_HEADS, ATT_DH).astype(F32)
    outs = post([xp, xs_], [op, os_.reshape(bs * ts, D_MODEL)], l1_att_w_out, l1_norm_ffn, l1_ffn_w_in, l1_ffn_w_out)
    streams = [(outs[0], bp, tp), (outs[1], bs, ts)]

    w_lru = l2_lru_w_in.astype(BF)
    wr = _group_block_diag(l2_lru_w_rgate)
    wi = _group_block_diag(l2_lru_w_igate)
    lru_states = [(jnp.zeros((bp, 1, LRU_W), F32), jnp.zeros((bp, CONV_W - 1, LRU_W), F32)),
                  (state_l2_h.reshape(bs, 1, LRU_W).astype(F32), state_l2_conv.astype(F32))]
    mixed, lru_new = [], []
    for (x, b, t), (h0, cbuf) in zip(streams, lru_states):
        gate, xr = _split_proj(x, _row(l2_norm_mix), w_lru, [LRU_W] * 2, [BF, F32], _row_tile(x.shape[0], tm_proj), "lru_proj")
        y, hfin, cfin = _lru_mix(xr, gate, h0, cbuf, l2_lru_conv_w.astype(F32), _row(l2_lru_conv_b), wr,
                                 _row(l2_lru_b_rgate), wi, _row(l2_lru_b_igate), _row(l2_lru_lambda), b, t)
        mixed.append(y)
        lru_new.append((hfin.reshape(b, LRU_W), cfin))
    outs = post([x for x, _, _ in streams], mixed, l2_lru_w_out, l2_norm_ffn, l2_ffn_w_in, l2_ffn_w_out)
    streams = [(outs[0], bp, tp), (outs[1], bs, ts)]
    (h2_p, conv2_p), (h2_s, conv2_s) = lru_new

    streams, (gla3_p, gla3_s) = gla_layer(
        streams, [zeros_gla, state_l3_gla], l3_norm_mix, l3_gla_w_in, l3_gla_w_gate_a, l3_gla_w_gate_b,
        l3_gla_b_gate, l3_gla_head_norm, l3_gla_w_out, l3_norm_ffn, l3_ffn_w_in, l3_ffn_w_out, fn=final_norm)

    y_prompt = streams[0][0].reshape(bp, tp, D_MODEL)
    y_sample = streams[1][0].reshape(bs, ts, D_MODEL)
    return (y_prompt, y_sample, gla0_p, gla0_s, k1_p, v1_p, k1_s, v1_s,
            h2_p, conv2_p, h2_s, conv2_s, gla3_p, gla3_s)
```

```python
import functools
import math

import numpy as np
import jax
import jax.numpy as jnp
from jax import lax
from jax.experimental import pallas as pl
from jax.experimental.pallas import tpu as pltpu

F32 = jnp.float32
BF = jnp.bfloat16

D_MODEL = 1024
CHUNK = 64
EPS = 1e-6
NEG_INF = -1e30
LOG2E = 1.4426950408889634

GLA_HEADS = 4
GLA_DK = 128
GLA_DV = 256
GLA_RANK = 16
GLA_TAU = 16.0
GLA_HK = GLA_HEADS * GLA_DK
GLA_HV = GLA_HEADS * GLA_DV
GLA_TILE_CHUNKS = 4
GLA_ROWS_PER_STEP = 2

ATT_HEADS = 16
ATT_DH = 64
ATT_BAND = 512
ATT_MAX_REL = 128
ATT_QT = 128
ATT_KW = ATT_BAND + ATT_QT
ATT_BLK = 512
ATT_TBL = 384
ATT_LOOKAHEAD = 1

LRU_W = 1536
LRU_BLOCK = 96
LRU_GROUP = 384
LRU_C = 8.0
CONV_W = 4

D_FF = 2816

LANE = 128
TM_PROJ = 1024
TM_POST = 512
VMEM_LIMIT_BYTES = 56 * 1024 * 1024


def _cparams(*sem):
    return pltpu.CompilerParams(dimension_semantics=sem, vmem_limit_bytes=VMEM_LIMIT_BYTES)


def _resident(shape):
    nd = len(shape)
    return pl.BlockSpec(shape, lambda *_: (0,) * nd, pipeline_mode=pl.Buffered(1))


def _dot(a, b):
    return jnp.dot(a, b, preferred_element_type=F32)


def _dot_nt(a, b):
    return lax.dot_general(a, b, (((1,), (1,)), ((), ())), preferred_element_type=F32)


def _dot_tn(a, b):
    return lax.dot_general(a, b, (((0,), (0,)), ((), ())), preferred_element_type=F32)


def _rms(x, g):
    return x * lax.rsqrt(jnp.mean(x * x, axis=-1, keepdims=True) + EPS) * g


def _log_sigmoid(z):
    return jnp.minimum(z, 0.0) - jnp.log1p(jnp.exp(-jnp.abs(z)))


def _silu(x):
    return x * jax.nn.sigmoid(x)


def _gelu_tanh(x):
    return x * (0.5 * (1.0 + jnp.tanh(math.sqrt(2.0 / math.pi) * (x + 0.044715 * (x * x * x)))))


def _row_tile(m, pref):
    t = min(m, pref)
    assert m % t == 0
    return t


def _gla_proj_kernel(x_ref, gn_ref, w_ref, wa_ref, wb_ref, bg_ref, q_ref, k_ref, v_ref, r_ref, g_ref):
    h = _rms(x_ref[...], gn_ref[...]).astype(BF)
    low = _dot(h, wa_ref[...]).astype(BF)
    z = _dot(low, wb_ref[...]) + bg_ref[...]
    q_ref[...] = _dot(h, w_ref[:, 0:GLA_HK]).astype(BF)
    k_ref[...] = _dot(h, w_ref[:, GLA_HK:2 * GLA_HK]).astype(BF)
    g_ref[...] = _log_sigmoid(z) * (1.0 / GLA_TAU)
    v_ref[...] = _dot(h, w_ref[:, 2 * GLA_HK:2 * GLA_HK + GLA_HV]).astype(BF)
    r_ref[...] = _dot(h, w_ref[:, 2 * GLA_HK + GLA_HV:]).astype(BF)


def _gla_proj(x2, gn, w_in, wa, wb, bg, tm):
    m = x2.shape[0]
    row = lambda n: pl.BlockSpec((tm, n), lambda i: (i, 0))
    return pl.pallas_call(
        _gla_proj_kernel,
        grid=(m // tm,),
        in_specs=[row(D_MODEL), _resident(gn.shape), _resident(w_in.shape), _resident(wa.shape),
                  _resident(wb.shape), _resident(bg.shape)],
        out_specs=[row(GLA_HK), row(GLA_HK), row(GLA_HV), row(GLA_HV), row(GLA_HK)],
        out_shape=[jax.ShapeDtypeStruct((m, GLA_HK), BF), jax.ShapeDtypeStruct((m, GLA_HK), BF),
                   jax.ShapeDtypeStruct((m, GLA_HV), BF), jax.ShapeDtypeStruct((m, GLA_HV), BF),
                   jax.ShapeDtypeStruct((m, GLA_HK), F32)],
        compiler_params=_cparams("parallel"),
        name="gla_proj",
    )(x2, gn, w_in, wa, wb, bg)


def _split_proj_kernel(x_ref, gn_ref, w_ref, *out_refs, epilogues):
    h = _rms(x_ref[...], gn_ref[...]).astype(BF)
    col = 0
    for o_ref, epilogue in zip(out_refs, epilogues):
        n = o_ref.shape[1]
        acc = _dot(h, w_ref[:, col:col + n])
        o_ref[...] = (acc if epilogue is None else epilogue(acc)).astype(o_ref.dtype)
        col += n


def _split_proj(x2, gn, w, widths, dtypes, tm, name, epilogues=None):
    epilogues = tuple(epilogues) if epilogues is not None else (None,) * len(widths)
    m = x2.shape[0]
    row = lambda n: pl.BlockSpec((tm, n), lambda i: (i, 0))
    return pl.pallas_call(
        functools.partial(_split_proj_kernel, epilogues=epilogues),
        grid=(m // tm,),
        in_specs=[row(D_MODEL), _resident(gn.shape), _resident(w.shape)],
        out_specs=[row(n) for n in widths],
        out_shape=[jax.ShapeDtypeStruct((m, n), dt) for n, dt in zip(widths, dtypes)],
        compiler_params=_cparams("parallel"),
        name=name,
    )(x2, gn, w)


def _post_kernel(*refs, final):
    if final:
        a_ref, x_ref, wo_ref, gn_ref, w1_ref, w2_ref, fn_ref, out_ref = refs
    else:
        a_ref, x_ref, wo_ref, gn_ref, w1_ref, w2_ref, out_ref = refs
    x1 = x_ref[...] + _dot(a_ref[...], wo_ref[...])
    h = _rms(x1, gn_ref[...]).astype(BF)
    g = _dot(h, w1_ref[:, 0:D_FF])
    u = _dot(h, w1_ref[:, D_FF:])
    act = (_silu(g) * u).astype(BF)
    x2 = x1 + _dot(act, w2_ref[...])
    if final:
        x2 = _rms(x2, fn_ref[...])
    out_ref[...] = x2


def _post(a2, x2, wo, gn, w1, w2, fn, tm):
    m = x2.shape[0]
    ka = a2.shape[1]
    final = fn is not None
    row = lambda n: pl.BlockSpec((tm, n), lambda i: (i, 0))
    ins = [a2, x2, wo, gn, w1, w2] + ([fn] if final else [])
    specs = [row(ka), row(D_MODEL)] + [_resident(t.shape) for t in ins[2:]]
    return pl.pallas_call(
        functools.partial(_post_kernel, final=final),
        grid=(m // tm,),
        in_specs=specs,
        out_specs=row(D_MODEL),
        out_shape=jax.ShapeDtypeStruct((m, D_MODEL), F32),
        compiler_params=_cparams("parallel"),
        name="post_final" if final else "post",
    )(*ins)


def _gla_stream(sb, q_ref, k_ref, v_ref, r_ref, g_ref, hn, s_scr, o_ref, tril, a_mask, chunk):
    tt = q_ref.shape[1]
    n_chunks = tt // chunk
    heads = [(slice(hd * GLA_DK, (hd + 1) * GLA_DK), slice(hd * GLA_DV, (hd + 1) * GLA_DV))
             for hd in range(GLA_HEADS)]
    rows = [slice(c * chunk, (c + 1) * chunk) for c in range(n_chunks)]

    g = g_ref[sb]
    g_hi = g.astype(BF)
    g_lo = (g - g_hi.astype(F32)).astype(BF)
    b = _dot(tril, g_hi) + _dot(tril, g_lo)
    yield

    b_last = [b[rs.stop - 1:rs.stop, :] for rs in rows]
    b_end = jnp.concatenate([jnp.broadcast_to(bl, (chunk, GLA_HK)) for bl in b_last], axis=0)
    k = k_ref[sb].astype(F32)
    q_e = ((q_ref[sb].astype(F32) * (GLA_DK ** -0.5)) * jnp.exp(b)).astype(BF)
    k_e = (k * jnp.exp(-b)).astype(BF)
    k_s = (k * jnp.exp(b_end - b)).astype(BF)
    scores = [_dot_nt(q_e[:, ks], k_e[:, ks]) for ks, _ in heads]
    kv = [[_dot_tn(k_s[rs, ks], v_ref[sb, rs, vs]) for ks, vs in heads] for rs in rows]
    yield

    o_intra = [_dot(jnp.where(a_mask, sc, 0.0).astype(BF), v_ref[sb, :, vs]) for sc, (_, vs) in zip(scores, heads)]
    state = [s_scr[sb, hd] for hd in range(GLA_HEADS)]
    o_inter = []
    for c, rs in enumerate(rows):
        o_inter.append([_dot(q_e[rs, ks], state[hd].astype(BF)) for hd, (ks, _) in enumerate(heads)])
        decay = jnp.transpose(jnp.broadcast_to(jnp.exp(b_last[c]), (LANE, GLA_HK)))
        for hd, (ks, _) in enumerate(heads):
            d = decay[ks, :]
            state[hd] = jnp.concatenate([d, d], axis=1) * state[hd] + kv[c][hd]
    for hd in range(GLA_HEADS):
        s_scr[sb, hd] = state[hd]
    yield

    for hd, (_, vs) in enumerate(heads):
        o = o_intra[hd] + jnp.concatenate([o_inter[c][hd] for c in range(n_chunks)], axis=0)
        o = o * lax.rsqrt(jnp.mean(o * o, axis=-1, keepdims=True) + EPS) * hn
        o_ref[sb, :, vs] = (o * _silu(r_ref[sb, :, vs].astype(F32))).astype(BF)
    yield


def _gla_kernel(q_ref, k_ref, v_ref, r_ref, g_ref, s0_ref, hn_ref, o_ref, sfin_ref, s_scr, *, chunk):
    i = pl.program_id(1)
    nb, tt = q_ref.shape[0], q_ref.shape[1]

    @pl.when(i == 0)
    def _():
        s_scr[...] = s0_ref[...]

    t_row = lax.broadcasted_iota(jnp.int32, (tt, tt), 0)
    t_col = lax.broadcasted_iota(jnp.int32, (tt, tt), 1)
    a_mask = t_row >= t_col
    for c in range(1, tt // chunk):
        a_mask = a_mask & ((t_row < c * chunk) | (t_col >= c * chunk))
    tril = jnp.where(a_mask, 1.0, 0.0).astype(BF)
    hn = hn_ref[...]

    streams = [_gla_stream(sb, q_ref, k_ref, v_ref, r_ref, g_ref, hn, s_scr, o_ref, tril, a_mask, chunk)
               for sb in range(nb)]
    for _ in range(4):
        for st in streams:
            next(st)

    @pl.when(i == pl.num_programs(1) - 1)
    def _():
        sfin_ref[...] = s_scr[...]


def _gla_mix(q, k, v, r, g, s0, hn, batch, seq):
    chunk = min(seq, CHUNK)
    tt = min(seq, GLA_TILE_CHUNKS * chunk)
    nb = GLA_ROWS_PER_STEP if batch % GLA_ROWS_PER_STEP == 0 else 1
    assert seq % tt == 0 and tt % chunk == 0
    blk = lambda n: pl.BlockSpec((nb, tt, n), lambda b, i: (b, i, 0))
    st = pl.BlockSpec((nb, GLA_HEADS, GLA_DK, GLA_DV), lambda b, i: (b, 0, 0, 0))
    as3 = lambda t: t.reshape(batch, seq, t.shape[-1])
    o, s_new = pl.pallas_call(
        functools.partial(_gla_kernel, chunk=chunk),
        grid=(batch // nb, seq // tt),
        in_specs=[blk(GLA_HK), blk(GLA_HK), blk(GLA_HV), blk(GLA_HV), blk(GLA_HK), st,
                  pl.BlockSpec(hn.shape, lambda b, i: (0, 0))],
        out_specs=[blk(GLA_HV), st],
        out_shape=[jax.ShapeDtypeStruct((batch, seq, GLA_HV), BF),
                   jax.ShapeDtypeStruct((batch, GLA_HEADS, GLA_DK, GLA_DV), F32)],
        scratch_shapes=[pltpu.VMEM((nb, GLA_HEADS, GLA_DK, GLA_DV), F32)],
        compiler_params=_cparams("parallel", "arbitrary"),
        name="gla_mix",
    )(as3(q), as3(k), as3(v), as3(r), as3(g), s0, hn)
    return o.reshape(batch * seq, GLA_HV), s_new


def _bias_kernel(idx_ref, madd_ref, thi_ref, tmid_ref, tlo_ref, out_ref):
    d = lax.broadcasted_iota(jnp.int32, (ATT_TBL, idx_ref.shape[1]), 0)
    onehot = jnp.where(d == idx_ref[...], 1.0, 0.0).astype(BF)
    b = _dot(thi_ref[...], onehot) + _dot(tmid_ref[...], onehot) + _dot(tlo_ref[...], onehot)
    out_ref[...] = b * LOG2E + madd_ref[...]


def _rel_bias(table, nq, visible):
    t = np.arange(nq)[:, None]
    r = np.arange(ATT_KW)[None, :]
    idx = np.clip(ATT_BAND + t - r, -ATT_MAX_REL, ATT_MAX_REL) + ATT_MAX_REL
    idx = jnp.asarray(idx.reshape(1, nq * ATT_KW), jnp.int32)
    madd = jnp.asarray(np.where(visible(t, r), 0.0, NEG_INF).reshape(1, nq * ATT_KW), F32)
    tpad = jnp.pad(table.astype(F32), ((0, 0), (0, ATT_TBL - table.shape[1])))
    t_hi = tpad.astype(BF)
    t_mid = (tpad - t_hi.astype(F32)).astype(BF)
    t_lo = (tpad - t_hi.astype(F32) - t_mid.astype(F32)).astype(BF)
    cols = 2 * ATT_KW
    tab = pl.BlockSpec((ATT_HEADS, ATT_TBL), lambda i: (0, 0))
    vec = pl.BlockSpec((1, cols), lambda i: (0, i))
    out = pl.pallas_call(
        _bias_kernel,
        grid=(nq * ATT_KW // cols,),
        in_specs=[vec, vec, tab, tab, tab],
        out_specs=pl.BlockSpec((ATT_HEADS, cols), lambda i: (0, i)),
        out_shape=jax.ShapeDtypeStruct((ATT_HEADS, nq * ATT_KW), F32),
        compiler_params=_cparams("parallel"),
        name="rel_bias",
    )(idx, madd, t_hi, t_mid, t_lo)
    return out.reshape(ATT_HEADS, nq, ATT_KW)


def _att_scores(q2, kw, sub):
    lane = lax.broadcasted_iota(jnp.int32, q2.shape, 1)
    mine = (lane < ATT_DH) if sub == 0 else (lane >= ATT_DH)
    return _dot_nt(jnp.where(mine, q2, jnp.zeros_like(q2)), kw)


def _att_probs(s, bias2, visible):
    s2 = s * LOG2E + bias2
    if visible is not None:
        s2 = jnp.where(visible, s2, NEG_INF)
    p = jnp.exp2(s2 - jnp.max(s2, axis=-1, keepdims=True))
    return p.astype(BF), 1.0 / jnp.sum(p, axis=-1, keepdims=True)


def _att_pipeline(items, scores, probs, finish):
    n = len(items)
    queue = [scores(items[j]) for j in range(min(ATT_LOOKAHEAD, n))]
    pending = None
    for j, item in enumerate(items):
        if j + ATT_LOOKAHEAD < n:
            queue.append(scores(items[j + ATT_LOOKAHEAD]))
        p_inv = probs(item, queue.pop(0))
        if pending is not None:
            finish(*pending)
        pending = (item, p_inv)
    finish(*pending)


def _attn_prompt_kernel(q_ref, kp_ref, kc_ref, vp_ref, vc_ref, bias_ref, o_ref):
    items = [(jj, hp) for jj in range(ATT_BLK // ATT_QT) for hp in range(ATT_HEADS // 2)]
    r = lax.broadcasted_iota(jnp.int32, (2 * ATT_QT, ATT_KW), 1)
    lane = lax.broadcasted_iota(jnp.int32, (ATT_QT, LANE), 1)

    def window(prev_ref, cur_ref, jj, hp):
        lo, ls = jj * ATT_QT, slice(hp * LANE, (hp + 1) * LANE)
        return jnp.concatenate([prev_ref[lo:, ls], cur_ref[0:lo + ATT_QT, ls]], axis=0)

    def scores(item):
        jj, hp = item
        q2 = q_ref[jj * ATT_QT:(jj + 1) * ATT_QT, hp * LANE:(hp + 1) * LANE]
        zero = jnp.zeros_like(q2)
        q_pair = jnp.concatenate([jnp.where(lane < ATT_DH, q2, zero), jnp.where(lane >= ATT_DH, q2, zero)], axis=0)
        return _dot_nt(q_pair, window(kp_ref, kc_ref, jj, hp))

    def body(first_block):
        def probs(item, s):
            jj, hp = item
            visible = (r >= ATT_BLK - jj * ATT_QT) if first_block else None
            bias2 = jnp.concatenate([bias_ref[2 * hp], bias_ref[2 * hp + 1]], axis=0)
            return _att_probs(s, bias2, visible)

        def finish(item, p_inv):
            jj, hp = item
            o = _dot(p_inv[0], window(vp_ref, vc_ref, jj, hp)) * p_inv[1]
            o_ref[jj * ATT_QT:(jj + 1) * ATT_QT, hp * LANE:(hp + 1) * LANE] = (
                jnp.where(lane < ATT_DH, o[0:ATT_QT, :], o[ATT_QT:, :]).astype(BF))

        _att_pipeline(items, scores, probs, finish)

    i = pl.program_id(1)
    pl.when(i == 0)(lambda: body(True))
    pl.when(i > 0)(lambda: body(False))


def _attn_prompt(q, k, v, bias, batch, seq):
    nt = seq // ATT_BLK
    assert seq % ATT_BLK == 0
    cur = pl.BlockSpec((ATT_BLK, D_MODEL), lambda b, i: (b * nt + i, 0))
    prev = pl.BlockSpec((ATT_BLK, D_MODEL), lambda b, i: (b * nt + jnp.maximum(i - 1, 0), 0))
    return pl.pallas_call(
        _attn_prompt_kernel,
        grid=(batch, nt),
        in_specs=[cur, prev, cur, prev, cur, _resident(bias.shape)],
        out_specs=cur,
        out_shape=jax.ShapeDtypeStruct((batch * seq, D_MODEL), BF),
        compiler_params=_cparams("parallel", "arbitrary"),
        name="attn_prompt",
    )(q, k, k, v, v, bias)


def _attn_sample_kernel(q_ref, kw_ref, vw_ref, bias_ref, o_ref):
    nq = q_ref.shape[1]
    items = [(hp, sub) for hp in range(ATT_HEADS // 2) for sub in range(2)]
    lane = lax.broadcasted_iota(jnp.int32, (nq, LANE), 1)
    held = {}

    def scores(item):
        hp, sub = item
        ls = slice(hp * LANE, (hp + 1) * LANE)
        return _att_scores(q_ref[0, :, ls], kw_ref[0, :, ls], sub)

    def probs(item, s):
        hp, sub = item
        return _att_probs(s, bias_ref[2 * hp + sub], None)

    def finish(item, p_inv):
        hp, sub = item
        ls = slice(hp * LANE, (hp + 1) * LANE)
        o = _dot(p_inv[0], vw_ref[0, :, ls]) * p_inv[1]
        if sub == 0:
            held[0] = o
        else:
            o_ref[0, :, ls] = jnp.where(lane < ATT_DH, held[0], o).astype(BF)

    _att_pipeline(items, scores, probs, finish)


def _attn_sample(q3, kw3, vw3, bias):
    batch, nq, _ = q3.shape
    blk = lambda n: pl.BlockSpec((1, n, D_MODEL), lambda b: (b, 0, 0))
    return pl.pallas_call(
        _attn_sample_kernel,
        grid=(batch,),
        in_specs=[blk(nq), blk(ATT_KW), blk(ATT_KW), _resident(bias.shape)],
        out_specs=blk(nq),
        out_shape=jax.ShapeDtypeStruct((batch, nq, D_MODEL), BF),
        compiler_params=_cparams("parallel"),
        name="attn_sample",
    )(q3, kw3, vw3, bias)


def _lru_kernel(xr_ref, gate_ref, h0_ref, cb_ref, cw_ref, cbias_ref, wr_ref, br_ref, wi_ref, bi_ref, lam_ref,
                y_ref, hfin_ref, cfin_ref, xs_scr, h_scr, *, tt):
    i = pl.program_id(1)

    @pl.when(i == 0)
    def _():
        xs_scr[0:8, :] = jnp.zeros((8, LRU_W), F32)
        xs_scr[8 - (CONV_W - 1):8, :] = cb_ref[0]
        h_scr[...] = h0_ref[0]

    xs_scr[8:, :] = xr_ref[...]
    cw = cw_ref[...]
    xc = cbias_ref[...] + cw[CONV_W - 1:CONV_W, :] * xs_scr[8:, :]
    for j in range(CONV_W - 1):
        xc = xc + cw[j:j + 1, :] * xs_scr[pl.ds(8 - (CONV_W - 1) + j, tt), :]
    xcb = xc.astype(BF)
    rg, ig = [], []
    for gi in range(LRU_W // LRU_GROUP):
        gs = slice(gi * LRU_GROUP, (gi + 1) * LRU_GROUP)
        rg.append(_dot(xcb[:, gs], wr_ref[gi]))
        ig.append(_dot(xcb[:, gs], wi_ref[gi]))
    rgate = jax.nn.sigmoid(jnp.concatenate(rg, axis=1) + br_ref[...])
    igate = jax.nn.sigmoid(jnp.concatenate(ig, axis=1) + bi_ref[...])
    log_a = LRU_C * rgate * _log_sigmoid(lam_ref[...])
    a = jnp.exp(log_a)
    u = jnp.sqrt(-jnp.tanh(log_a) * (a * a + 1.0)) * (igate * xc)

    carry = h_scr[...]
    groups = []
    for gi in range(tt // 8):
        ag, ug = a[8 * gi:8 * gi + 8, :], u[8 * gi:8 * gi + 8, :]
        for step in (1, 2, 4):
            a_sh = jnp.concatenate([jnp.ones((step, LRU_W), F32), ag[0:8 - step, :]], axis=0)
            u_sh = jnp.concatenate([jnp.zeros((step, LRU_W), F32), ug[0:8 - step, :]], axis=0)
            ug = ag * u_sh + ug
            ag = ag * a_sh
        hg = ag * carry + ug
        carry = hg[7:8, :]
        groups.append(hg)
    hs = jnp.concatenate(groups, axis=0)
    h_scr[...] = carry
    y_ref[...] = (hs * gate_ref[...].astype(F32)).astype(BF)
    xs_scr[0:8, :] = xs_scr[tt:tt + 8, :]

    @pl.when(i == pl.num_programs(1) - 1)
    def _():
        hfin_ref[0] = carry
        cfin_ref[0] = xs_scr[tt + 8 - (CONV_W - 1):tt + 8, :]


def _lru_mix(xr, gate, h0, cbuf, cw, cbias, wr, br, wi, bi, lam, batch, seq):
    tt = min(seq, 256)
    nt = seq // tt
    assert seq % tt == 0 and tt % 8 == 0
    row = lambda n: pl.BlockSpec((tt, n), lambda b, i: (b * nt + i, 0))
    per_b = lambda n: pl.BlockSpec((1, n, LRU_W), lambda b, i: (b, 0, 0))
    const = lambda t: pl.BlockSpec(t.shape, lambda b, i: (0,) * t.ndim)
    return pl.pallas_call(
        functools.partial(_lru_kernel, tt=tt),
        grid=(batch, nt),
        in_specs=[row(LRU_W), row(LRU_W), per_b(1), per_b(CONV_W - 1), const(cw), const(cbias),
                  const(wr), const(br), const(wi), const(bi), const(lam)],
        out_specs=[row(LRU_W), per_b(1), per_b(CONV_W - 1)],
        out_shape=[jax.ShapeDtypeStruct((batch * seq, LRU_W), BF),
                   jax.ShapeDtypeStruct((batch, 1, LRU_W), F32),
                   jax.ShapeDtypeStruct((batch, CONV_W - 1, LRU_W), F32)],
        scratch_shapes=[pltpu.VMEM((tt + 8, LRU_W), F32), pltpu.VMEM((1, LRU_W), F32)],
        compiler_params=_cparams("parallel", "arbitrary"),
        name="lru_mix",
    )(xr, gate, h0, cbuf, cw, cbias, wr, br, wi, bi, lam)


def _group_block_diag(w):
    per = LRU_GROUP // LRU_BLOCK
    wg = w.reshape(LRU_W // LRU_GROUP, per, LRU_BLOCK, LRU_BLOCK)
    eye = jnp.eye(per, dtype=w.dtype)
    out = jnp.einsum('gpij,pq->gpiqj', wg, eye)
    return out.reshape(LRU_W // LRU_GROUP, LRU_GROUP, LRU_GROUP).astype(BF)


def _row(v):
    return v.reshape(1, -1).astype(F32)


def kernel(x_prompt, x_sample, state_l0_gla, cache_l1_k, cache_l1_v, state_l2_h, state_l2_conv, state_l3_gla, l0_norm_mix, l0_gla_w_in, l0_gla_w_gate_a, l0_gla_w_gate_b, l0_gla_b_gate, l0_gla_head_norm, l0_gla_w_out, l0_norm_ffn, l0_ffn_w_in, l0_ffn_w_out, l1_norm_mix, l1_att_w_qkv, l1_att_rel_bias, l1_att_w_out, l1_norm_ffn, l1_ffn_w_in, l1_ffn_w_out, l2_norm_mix, l2_lru_w_in, l2_lru_conv_w, l2_lru_conv_b, l2_lru_w_rgate, l2_lru_b_rgate, l2_lru_w_igate, l2_lru_b_igate, l2_lru_lambda, l2_lru_w_out, l2_norm_ffn, l2_ffn_w_in, l2_ffn_w_out, l3_norm_mix, l3_gla_w_in, l3_gla_w_gate_a, l3_gla_w_gate_b, l3_gla_b_gate, l3_gla_head_norm, l3_gla_w_out, l3_norm_ffn, l3_ffn_w_in, l3_ffn_w_out, final_norm):
    bp, tp, _ = x_prompt.shape
    bs, ts, _ = x_sample.shape
    streams = [(x_prompt.reshape(bp * tp, D_MODEL), bp, tp), (x_sample.reshape(bs * ts, D_MODEL), bs, ts)]
    tm_proj = TM_PROJ
    tm_post = TM_POST

    def post(xs, mixed, wo, gn, w1, w2, fn=None):
        wo_b, w1_b, w2_b = wo.astype(BF), w1.astype(BF), w2.astype(BF)
        fn_r = None if fn is None else _row(fn)
        return [_post(a, x, wo_b, _row(gn), w1_b, w2_b, fn_r, _row_tile(x.shape[0], tm_post))
                for a, x in zip(mixed, xs)]

    def gla_layer(xs, states, norm_mix, w_in, wa, wb, bg, hn, w_out, norm_ffn, f_in, f_out, fn=None):
        wa_p = jnp.pad(wa, ((0, 0), (0, LANE - GLA_RANK))).astype(BF)
        wb_p = jnp.pad(wb, ((0, LANE - GLA_RANK), (0, 0))).astype(BF)
        w_in_b = w_in.astype(BF)
        mixed, new_states = [], []
        for (x, b, t), s0 in zip(xs, states):
            q, k, v, r, g = _gla_proj(x, _row(norm_mix), w_in_b, wa_p, wb_p, _row(bg), _row_tile(x.shape[0], tm_proj))
            o, s_new = _gla_mix(q, k, v, r, g, s0.astype(F32), _row(hn), b, t)
            mixed.append(o)
            new_states.append(s_new)
        outs = post([x for x, _, _ in xs], mixed, w_out, norm_ffn, f_in, f_out, fn)
        return [(o, b, t) for o, (_, b, t) in zip(outs, xs)], new_states

    zeros_gla = jnp.zeros((bp, GLA_HEADS, GLA_DK, GLA_DV), F32)
    streams, (gla0_p, gla0_s) = gla_layer(
        streams, [zeros_gla, state_l0_gla], l0_norm_mix, l0_gla_w_in, l0_gla_w_gate_a, l0_gla_w_gate_b,
        l0_gla_b_gate, l0_gla_head_norm, l0_gla_w_out, l0_norm_ffn, l0_ffn_w_in, l0_ffn_w_out)

    n_cache = cache_l1_k.shape[1]
    n_keys = n_cache + ts
    assert n_cache == ATT_BAND and n_keys <= ATT_KW and ts <= ATT_QT
    bias_p = _rel_bias(l1_att_rel_bias, ATT_QT,
                       lambda t, r: ((t < CHUNK) & (r < ATT_BAND + CHUNK)) | ((t >= CHUNK) & (r >= CHUNK)))
    bias_s = _rel_bias(l1_att_rel_bias, ts, lambda t, r: (r < n_keys) & (t >= 0))
    w_qkv = l1_att_w_qkv.astype(BF)
    (xp, _, _), (xs_, _, _) = streams
    qkv_epilogues = (lambda acc: acc * (ATT_DH ** -0.5), None, None)
    qp, kp, vp = _split_proj(xp, _row(l1_norm_mix), w_qkv, [D_MODEL] * 3, [BF] * 3, _row_tile(bp * tp, tm_proj),
                             "qkv_proj", qkv_epilogues)
    qs, ks, vs = _split_proj(xs_, _row(l1_norm_mix), w_qkv, [D_MODEL] * 3, [BF] * 3, _row_tile(bs * ts, tm_proj),
                             "qkv_proj", qkv_epilogues)
    op = _attn_prompt(qp, kp, vp, bias_p, bp, tp)

    def window(cache, new):
        c = cache.reshape(bs, n_cache, D_MODEL).astype(BF)
        z = jnp.zeros((bs, ATT_KW - n_keys, D_MODEL), BF)
        return jnp.concatenate([c, new.reshape(bs, ts, D_MODEL), z], axis=1)

    os_ = _attn_sample(qs.reshape(bs, ts, D_MODEL), window(cache_l1_k, ks), window(cache_l1_v, vs), bias_s)
    keep = min(ATT_BAND, tp)
    k1_p = kp.reshape(bp, tp, D_MODEL)[:, tp - keep:].astype(F32).reshape(bp, keep, ATT_HEADS, ATT_DH)
    v1_p = vp.reshape(bp, tp, D_MODEL)[:, tp - keep:].astype(F32).reshape(bp, keep, ATT_HEADS, ATT_DH)
    k1_s = ks.reshape(bs, ts, ATT_HEADS, ATT_DH).astype(F32)
    v1_s = vs.reshape(bs, ts, ATT_HEADS, ATT_DH).astype(F32)
    outs = post([xp, xs_], [op, os_.reshape(bs * ts, D_MODEL)], l1_att_w_out, l1_norm_ffn, l1_ffn_w_in, l1_ffn_w_out)
    streams = [(outs[0], bp, tp), (outs[1], bs, ts)]

    w_lru = l2_lru_w_in.astype(BF)
    wr = _group_block_diag(l2_lru_w_rgate)
    wi = _group_block_diag(l2_lru_w_igate)
    lru_states = [(jnp.zeros((bp, 1, LRU_W), F32), jnp.zeros((bp, CONV_W - 1, LRU_W), F32)),
                  (state_l2_h.reshape(bs, 1, LRU_W).astype(F32), state_l2_conv.astype(F32))]
    mixed, lru_new = [], []
    for (x, b, t), (h0, cbuf) in zip(streams, lru_states):
        gate, xr = _split_proj(x, _row(l2_norm_mix), w_lru, [LRU_W] * 2, [BF, F32], _row_tile(x.shape[0], tm_proj),
                               "lru_proj", (_gelu_tanh, None))
        y, hfin, cfin = _lru_mix(xr, gate, h0, cbuf, l2_lru_conv_w.astype(F32), _row(l2_lru_conv_b), wr,
                                 _row(l2_lru_b_rgate), wi, _row(l2_lru_b_igate), _row(l2_lru_lambda), b, t)
        mixed.append(y)
        lru_new.append((hfin.reshape(b, LRU_W), cfin))
    outs = post([x for x, _, _ in streams], mixed, l2_lru_w_out, l2_norm_ffn, l2_ffn_w_in, l2_ffn_w_out)
    streams = [(outs[0], bp, tp), (outs[1], bs, ts)]
    (h2_p, conv2_p), (h2_s, conv2_s) = lru_new

    streams, (gla3_p, gla3_s) = gla_layer(
        streams, [zeros_gla, state_l3_gla], l3_norm_mix, l3_gla_w_in, l3_gla_w_gate_a, l3_gla_w_gate_b,
        l3_gla_b_gate, l3_gla_head_norm, l3_gla_w_out, l3_norm_ffn, l3_ffn_w_in, l3_ffn_w_out, fn=final_norm)

    y_prompt = streams[0][0].reshape(bp, tp, D_MODEL)
    y_sample = streams[1][0].reshape(bs, ts, D_MODEL)
    return (y_prompt, y_sample, gla0_p, gla0_s, k1_p, v1_p, k1_s, v1_s,
            h2_p, conv2_p, h2_s, conv2_s, gla3_p, gla3_s)
```
